```python
import math
import jax, jax.numpy as jnp
from jax import lax
import numpy as np

D_MODEL = 2048
BATCH = 1
SEQ = 8192
DEPTH = 1

ATT_WIDTH = D_MODEL // 2
V_HEAD_DIM = 128
N_ATT_HEADS = ATT_WIDTH // V_HEAD_DIM
QK_HEAD_DIM = V_HEAD_DIM // 2
Q_WIDTH = 2 * N_ATT_HEADS * QK_HEAD_DIM
K_WIDTH = Q_WIDTH
LAMBDA_INIT_BASE = 0.8
LAMBDA_INIT_AMP = 0.6
LAMBDA_INIT_RATE = 0.3
POOL_WIDTH = D_MODEL // 2
POOL_WINDOWS = (2, 4, 8, 16)
POOL_GROUPS = len(POOL_WINDOWS)
POOL_GROUP_WIDTH = POOL_WIDTH // POOL_GROUPS
IN_WIDTH = Q_WIDTH + K_WIDTH + ATT_WIDTH + POOL_WIDTH + 2 * D_MODEL
D_FF = 4 * D_MODEL
PLE_DIM = 256
Q_BLOCK = 128
NORM_EPS = 1e-6

kernel_name = "hybrid_diffattn_pool_gated_block"


def rms_norm(x, g):
    xf = x.astype(jnp.float32)
    y = xf * lax.rsqrt(jnp.mean(xf * xf, axis=-1, keepdims=True) + NORM_EPS)
    return (y * g.astype(jnp.float32)).astype(x.dtype)


def alibi_slopes(n_heads):
    return jnp.asarray(2.0 ** (-8.0 * np.arange(1, n_heads + 1) / n_heads), dtype=jnp.float32)


def diff_attention(q1, q2, k1, k2, v, lam):
    B, H, S, _ = k1.shape
    nb = S // Q_BLOCK
    scale = QK_HEAD_DIM ** -0.5
    slopes = alibi_slopes(H)
    key_pos = jnp.arange(S)

    def to_blocks(q):
        return (q * scale).reshape(B, H, nb, Q_BLOCK, -1).transpose(2, 0, 1, 3, 4)

    def one_block(args):
        qb1, qb2, start = args
        q_pos = start + jnp.arange(Q_BLOCK)
        dist = q_pos[:, None] - key_pos[None, :]
        bias = -slopes[:, None, None] * dist.astype(jnp.float32)
        causal = dist >= 0

        def probs(qb, k):
            s = jnp.einsum('bhqd,bhkd->bhqk', qb, k).astype(jnp.float32) + bias
            s = jnp.where(causal, s, -jnp.inf)
            return jax.nn.softmax(s, axis=-1)

        a = probs(qb1, k1) - lam * probs(qb2, k2)
        return jnp.einsum('bhqk,bhkv->bhqv', a.astype(v.dtype), v)

    starts = jnp.arange(nb) * Q_BLOCK
    out = lax.map(one_block, (to_blocks(q1), to_blocks(q2), starts))
    return out.transpose(1, 2, 0, 3, 4).reshape(B, H, S, V_HEAD_DIM)


def multiscale_pool(u):
    B, S, C = u.shape
    uf = u.astype(jnp.float32)
    csum = jnp.cumsum(uf, axis=1)
    t = jnp.arange(S)
    outs = []
    for g, w in enumerate(POOL_WINDOWS):
        sl = slice(g * POOL_GROUP_WIDTH, (g + 1) * POOL_GROUP_WIDTH)
        cg = csum[..., sl]
        lagged = jnp.pad(cg, ((0, 0), (w, 0), (0, 0)))[:, :S]
        count = jnp.minimum(t + 1, w).astype(jnp.float32)[None, :, None]
        outs.append((cg - lagged) / count - uf[..., sl])
    return jnp.stack(outs, axis=2).astype(u.dtype)


def setup_inputs(seed: int = 0) -> dict:
    key = jax.random.key(seed)
    ks = jax.random.split(key, 24)
    f32 = jnp.float32

    def w(k, shape, fan_in):
        return jax.random.normal(k, shape, f32) * (fan_in ** -0.5)

    def gain(k, shape):
        return 1.0 + 0.02 * jax.random.normal(k, shape, f32)

    return {
        "x": jax.random.normal(ks[0], (BATCH, SEQ, D_MODEL), f32),
        "p": jax.random.normal(ks[1], (DEPTH, BATCH, SEQ, PLE_DIM), f32),
        "norm_mix_g": gain(ks[2], (DEPTH, D_MODEL)),
        "w_in": w(ks[3], (DEPTH, D_MODEL, IN_WIDTH), D_MODEL),
        "lambda_q1": 0.1 * jax.random.normal(ks[4], (DEPTH, QK_HEAD_DIM), f32),
        "lambda_k1": 0.1 * jax.random.normal(ks[5], (DEPTH, QK_HEAD_DIM), f32),
        "lambda_q2": 0.1 * jax.random.normal(ks[6], (DEPTH, QK_HEAD_DIM), f32),
        "lambda_k2": 0.1 * jax.random.normal(ks[7], (DEPTH, QK_HEAD_DIM), f32),
        "subln_g": gain(ks[8], (DEPTH, V_HEAD_DIM)),
        "pool_grp_w": w(ks[9], (DEPTH, POOL_GROUPS, POOL_GROUP_WIDTH, POOL_GROUP_WIDTH), POOL_GROUP_WIDTH),
        "pool_scale": gain(ks[10], (DEPTH, POOL_WIDTH)),
        "w_attn_br": w(ks[11], (DEPTH, ATT_WIDTH, D_MODEL), ATT_WIDTH),
        "w_pool_br": w(ks[12], (DEPTH, POOL_WIDTH, D_MODEL), POOL_WIDTH),
        "w_out": w(ks[13], (DEPTH, D_MODEL, D_MODEL), D_MODEL),
        "norm_mlp_g": gain(ks[14], (DEPTH, D_MODEL)),
        "w_mlp_up": w(ks[15], (DEPTH, D_MODEL, D_FF), D_MODEL),
        "w_mlp_down": w(ks[16], (DEPTH, D_FF, D_MODEL), D_FF),
        "norm_ple_g": gain(ks[17], (DEPTH, D_MODEL)),
        "w_ple": w(ks[18], (DEPTH, PLE_DIM, D_MODEL), PLE_DIM),
        "w_ple_gate": w(ks[19], (DEPTH, D_MODEL, D_MODEL), D_MODEL),
        "final_norm_g": gain(ks[20], (D_MODEL,)),
    }


def reference(x, p, norm_mix_g, w_in, lambda_q1, lambda_k1, lambda_q2, lambda_k2, subln_g,
              pool_grp_w, pool_scale, w_attn_br, w_pool_br, w_out, norm_mlp_g, w_mlp_up,
              w_mlp_down, norm_ple_g, w_ple, w_ple_gate, final_norm_g):
    B, S, _ = x.shape
    H, dqk = N_ATT_HEADS, QK_HEAD_DIM
    o_q = 0
    o_k = o_q + Q_WIDTH
    o_v = o_k + K_WIDTH
    o_u = o_v + ATT_WIDTH
    o_ga = o_u + POOL_WIDTH
    o_gp = o_ga + D_MODEL
    for i in range(DEPTH):
        h = rms_norm(x, norm_mix_g[i])
        z = h @ w_in[i]
        q = z[..., o_q:o_k].reshape(B, S, 2, H, dqk).transpose(2, 0, 3, 1, 4)
        k = z[..., o_k:o_v].reshape(B, S, 2, H, dqk).transpose(2, 0, 3, 1, 4)
        v = z[..., o_v:o_u].reshape(B, S, H, V_HEAD_DIM).transpose(0, 2, 1, 3)
        u = z[..., o_u:o_ga]
        gate_a = jax.nn.sigmoid(z[..., o_ga:o_gp])
        gate_p = jax.nn.sigmoid(z[..., o_gp:])

        lam_init = LAMBDA_INIT_BASE - LAMBDA_INIT_AMP * math.exp(-LAMBDA_INIT_RATE * i)
        lam = (jnp.exp(jnp.sum(lambda_q1[i].astype(jnp.float32) * lambda_k1[i].astype(jnp.float32)))
               - jnp.exp(jnp.sum(lambda_q2[i].astype(jnp.float32) * lambda_k2[i].astype(jnp.float32)))
               + lam_init)
        o = diff_attention(q[0], q[1], k[0], k[1], v, lam)
        o = rms_norm(o, subln_g[i]) * (1.0 - lam_init)
        o = o.transpose(0, 2, 1, 3).reshape(B, S, ATT_WIDTH)
        a_branch = o @ w_attn_br[i]

        pooled = multiscale_pool(u)
        pm = jnp.einsum('bsgc,gcd->bsgd', pooled, pool_grp_w[i]).reshape(B, S, POOL_WIDTH)
        p_branch = (pm * pool_scale[i]) @ w_pool_br[i]

        merged = gate_a * a_branch + gate_p * p_branch
        x = x + merged @ w_out[i]

        hm = rms_norm(x, norm_mlp_g[i])
        x = x + jnp.square(jax.nn.relu(hm @ w_mlp_up[i])) @ w_mlp_down[i]

        ple_gate = jax.nn.sigmoid(rms_norm(x, norm_ple_g[i]) @ w_ple_gate[i])
        x = x + (p[i] @ w_ple[i]) * ple_gate
    return rms_norm(x, final_norm_g)
```

```python
import functools
import math

import numpy as np
import jax
import jax.numpy as jnp
from jax import lax
from jax.experimental import pallas as pl
from jax.experimental.pallas import tpu as pltpu

F32 = jnp.float32
BF16 = jnp.bfloat16

D_MODEL = 2048
SEQ = 8192
N_HEADS = 8
V_HEAD_DIM = 128
QK_HEAD_DIM = 64
ATT_WIDTH = N_HEADS * V_HEAD_DIM
POOL_WIDTH = 1024
POOL_WINDOWS = (2, 4, 8, 16)
POOL_GROUP_WIDTH = POOL_WIDTH // len(POOL_WINDOWS)
POOL_HALO = 16
D_FF = 4 * D_MODEL
PLE_DIM = 256
NORM_EPS = 1e-6
LAMBDA_INIT = 0.8 - 0.6 * math.exp(-0.3 * 0)
NEG_BIG = -1e30

ATT_TQ = 256
ATT_TK = 256
VMEM_LIMIT = 56 * 1024 * 1024


def _rms(xf, g):
    return xf * lax.rsqrt(jnp.mean(xf * xf, axis=-1, keepdims=True) + NORM_EPS) * g


def _dot(a, b):
    return jnp.dot(a, b, preferred_element_type=F32)


def _params(sem):
    return pltpu.CompilerParams(dimension_semantics=sem, vmem_limit_bytes=VMEM_LIMIT)


def _norm_qkv_kernel(x_ref, g_ref, wq_ref, wk_ref, wv_ref, h_ref, qT_ref, k_ref, vT_ref, *, bm):
    h = _rms(x_ref[...], g_ref[...]).astype(BF16)
    h_ref[...] = h
    q = _dot(h, wq_ref[...]) * (QK_HEAD_DIM ** -0.5)
    for hh in range(N_HEADS):
        for r in range(bm // ATT_TQ):
            blk = q[r * ATT_TQ:(r + 1) * ATT_TQ, hh * 128:(hh + 1) * 128]
            qT_ref[hh, r] = blk.T.astype(BF16)
    k_ref[...] = _dot(h, wk_ref[...]).astype(BF16)
    v = _dot(h, wv_ref[...])
    for hh in range(N_HEADS):
        for r in range(bm // ATT_TK):
            blk = v[r * ATT_TK:(r + 1) * ATT_TK, hh * 128:(hh + 1) * 128]
            vT_ref[hh, r] = blk.T.astype(BF16)


def _norm_qkv(x, g, wq, wk, wv, bm=512):
    S = x.shape[0]
    nq, nk = S // ATT_TQ, S // ATT_TK
    const = lambda i: (0, 0)
    return pl.pallas_call(
        functools.partial(_norm_qkv_kernel, bm=bm),
        grid=(S // bm,),
        in_specs=[
            pl.BlockSpec((bm, D_MODEL), lambda i: (i, 0)),
            pl.BlockSpec((1, D_MODEL), const),
            pl.BlockSpec((D_MODEL, 1024), const),
            pl.BlockSpec((D_MODEL, 1024), const),
            pl.BlockSpec((D_MODEL, 1024), const),
        ],
        out_specs=[
            pl.BlockSpec((bm, D_MODEL), lambda i: (i, 0)),
            pl.BlockSpec((N_HEADS, bm // ATT_TQ, 128, ATT_TQ), lambda i: (0, i, 0, 0)),
            pl.BlockSpec((bm, 1024), lambda i: (i, 0)),
            pl.BlockSpec((N_HEADS, bm // ATT_TK, 128, ATT_TK), lambda i: (0, i, 0, 0)),
        ],
        out_shape=[
            jax.ShapeDtypeStruct((S, D_MODEL), BF16),
            jax.ShapeDtypeStruct((N_HEADS, nq, 128, ATT_TQ), BF16),
            jax.ShapeDtypeStruct((S, 1024), BF16),
            jax.ShapeDtypeStruct((N_HEADS, nk, 128, ATT_TK), BF16),
        ],
        compiler_params=_params(("arbitrary",)),
        name="norm_qkv",
    )(x, g, wq, wk, wv)


def _ug_kernel(h_ref, w_ref, u_ref, gate_ref):
    j = pl.program_id(1)
    z = _dot(h_ref[...], w_ref[...])

    @pl.when(j == 0)
    def _():
        u_ref[...] = z

    @pl.when(j > 0)
    def _():
        gate_ref[...] = jax.nn.sigmoid(z).astype(BF16)


def _ug(h, w_ug, bm=1024, bn=1024):
    S = h.shape[0]
    n = w_ug.shape[1]
    return pl.pallas_call(
        _ug_kernel,
        grid=(S // bm, n // bn),
        in_specs=[
            pl.BlockSpec((bm, D_MODEL), lambda i, j: (i, 0)),
            pl.BlockSpec((D_MODEL, bn), lambda i, j: (0, j)),
        ],
        out_specs=[
            pl.BlockSpec((bm, POOL_WIDTH), lambda i, j: (i, 0)),
            pl.BlockSpec((bm, bn), lambda i, j: (i, jnp.maximum(j - 1, 0))),
        ],
        out_shape=[
            jax.ShapeDtypeStruct((S, POOL_WIDTH), F32),
            jax.ShapeDtypeStruct((S, 2 * D_MODEL), BF16),
        ],
        compiler_params=_params(("arbitrary", "arbitrary")),
        name="norm_ug",
    )(h, w_ug)


def _attn_kernel(slope_ref, qT_ref, k_ref, vT_ref, lq1_ref, lk1_ref, lq2_ref, lk2_ref, g_ref,
                 o_ref, qs_ref, acc_ref, m_ref, l_ref):
    h = pl.program_id(0)
    qi = pl.program_id(1)
    tq, tk = ATT_TQ, ATT_TK
    slope = slope_ref[h]

    qT = qT_ref[...]
    row = lax.broadcasted_iota(jnp.int32, (128, tq), 0)
    zero = jnp.zeros_like(qT)
    qs_ref[:, :tq] = jnp.where(row < QK_HEAD_DIM, qT, zero)
    qs_ref[:, tq:] = jnp.where(row >= QK_HEAD_DIM, qT, zero)

    acc_ref[...] = jnp.zeros_like(acc_ref)
    m_ref[...] = jnp.full_like(m_ref, NEG_BIG)
    l_ref[...] = jnp.zeros_like(l_ref)

    key_local = lax.broadcasted_iota(jnp.int32, (tk, 1), 0)
    colbias = slope * key_local.astype(F32)

    def step(j, masked):
        start = pl.multiple_of(j * tk, tk)
        k_blk = k_ref[pl.ds(start, tk), :]
        sT = _dot(k_blk, qs_ref[...]) + colbias
        if masked:
            kpos = lax.broadcasted_iota(jnp.int32, (tk, 2 * tq), 0)
            qcol = lax.broadcasted_iota(jnp.int32, (tk, 2 * tq), 1)
            qpos = jnp.where(qcol >= tq, qcol - tq, qcol)
            sT = jnp.where(kpos <= qpos, sT, NEG_BIG)
        off = slope * (j * tk - qi * tq).astype(F32)
        m_old = m_ref[...]
        m_new = jnp.maximum(m_old, jnp.max(sT, axis=0, keepdims=True) + off)
        alpha = jnp.exp(m_old - m_new)
        pT = jnp.exp(sT - (m_new - off))
        l_ref[...] = alpha * l_ref[...] + jnp.sum(pT, axis=0, keepdims=True)
        m_ref[...] = m_new
        acc_ref[...] = alpha * acc_ref[...] + _dot(vT_ref[j], pT.astype(BF16))

    def body(j, carry):
        step(j, False)
        return carry

    lax.fori_loop(0, qi, body, 0)
    step(qi, True)

    lam = (jnp.exp(jnp.sum(lq1_ref[...] * lk1_ref[...], axis=-1, keepdims=True))
           - jnp.exp(jnp.sum(lq2_ref[...] * lk2_ref[...], axis=-1, keepdims=True))
           + LAMBDA_INIT)
    acc = acc_ref[...]
    l = l_ref[...]
    oT = acc[:, :tq] / l[:, :tq] - lam * (acc[:, tq:] / l[:, tq:])
    o = oT.T
    o = _rms(o, g_ref[...]) * (1.0 - LAMBDA_INIT)
    o_ref[...] = o.astype(BF16)


def _attention(slopes, qT, k, vT, lq1, lk1, lq2, lk2, subln_g):
    S = k.shape[0]
    nq, nk = S // ATT_TQ, S // ATT_TK
    vec = lambda h, i: (0, 0)
    return pl.pallas_call(
        _attn_kernel,
        grid=(N_HEADS, nq),
        in_specs=[
            pl.BlockSpec(memory_space=pltpu.SMEM),
            pl.BlockSpec((None, None, 128, ATT_TQ), lambda h, i: (h, i, 0, 0)),
            pl.BlockSpec((S, 128), lambda h, i: (0, h)),
            pl.BlockSpec((None, nk, 128, ATT_TK), lambda h, i: (h, 0, 0, 0)),
            pl.BlockSpec((1, QK_HEAD_DIM), vec),
            pl.BlockSpec((1, QK_HEAD_DIM), vec),
            pl.BlockSpec((1, QK_HEAD_DIM), vec),
            pl.BlockSpec((1, QK_HEAD_DIM), vec),
            pl.BlockSpec((1, V_HEAD_DIM), vec),
        ],
        out_specs=pl.BlockSpec((ATT_TQ, 128), lambda h, i: (i, h)),
        out_shape=jax.ShapeDtypeStruct((S, ATT_WIDTH), BF16),
        scratch_shapes=[
            pltpu.VMEM((128, 2 * ATT_TQ), BF16),
            pltpu.VMEM((128, 2 * ATT_TQ), F32),
            pltpu.VMEM((1, 2 * ATT_TQ), F32),
            pltpu.VMEM((1, 2 * ATT_TQ), F32),
        ],
        compiler_params=_params(("arbitrary", "arbitrary")),
        name="diff_attn",
    )(slopes, qT, k, vT, lq1, lk1, lq2, lk2, subln_g)


def _mix_kernel(o_ref, u_ref, uprev_ref, gate_ref, x_ref, wg_ref, ps_ref, wa_ref, wp_ref, wo_ref,
                gm_ref, x1_ref, hm_ref, ext_ref, *, bm):
    i = pl.program_id(0)
    u = u_ref[...]
    halo = uprev_ref[...]
    ext_ref[0:POOL_HALO, :] = jnp.where(i == 0, jnp.zeros_like(halo), halo)
    ext_ref[POOL_HALO:POOL_HALO + bm, :] = u
    t = i * bm + lax.broadcasted_iota(jnp.int32, (bm, 1), 0)
    pms = []
    for g, w in enumerate(POOL_WINDOWS):
        c0, c1 = g * POOL_GROUP_WIDTH, (g + 1) * POOL_GROUP_WIDTH
        ug = u[:, c0:c1]
        win = ug
        for d in range(1, w):
            win = win + ext_ref[POOL_HALO - d:POOL_HALO - d + bm, c0:c1]
        count = jnp.minimum(t + 1, w).astype(F32)
        pooled = win / count - ug
        pms.append(_dot(pooled.astype(BF16), wg_ref[g]))
    pm = (jnp.concatenate(pms, axis=-1) * ps_ref[...]).astype(BF16)
    p_branch = _dot(pm, wp_ref[...])
    a_branch = _dot(o_ref[...], wa_ref[...])
    gates = gate_ref[...]
    merged = gates[:, :D_MODEL].astype(F32) * a_branch + gates[:, D_MODEL:].astype(F32) * p_branch
    x1 = x_ref[...] + _dot(merged.astype(BF16), wo_ref[...])
    x1_ref[...] = x1
    hm_ref[...] = _rms(x1, gm_ref[...]).astype(BF16)


def _mix(o, u, gates, x, wg, pool_scale, wa, wp, wo, g_mlp, bm=256):
    S = x.shape[0]
    const2 = lambda i: (0, 0)
    once = pl.Buffered(1)
    halo_blocks = bm // POOL_HALO
    return pl.pallas_call(
        functools.partial(_mix_kernel, bm=bm),
        grid=(S // bm,),
        in_specs=[
            pl.BlockSpec((bm, ATT_WIDTH), lambda i: (i, 0)),
            pl.BlockSpec((bm, POOL_WIDTH), lambda i: (i, 0)),
            pl.BlockSpec((POOL_HALO, POOL_WIDTH), lambda i: (jnp.maximum(i * halo_blocks - 1, 0), 0)),
            pl.BlockSpec((bm, 2 * D_MODEL), lambda i: (i, 0)),
            pl.BlockSpec((bm, D_MODEL), lambda i: (i, 0)),
            pl.BlockSpec((len(POOL_WINDOWS), POOL_GROUP_WIDTH, POOL_GROUP_WIDTH), lambda i: (0, 0, 0),
                         pipeline_mode=once),
            pl.BlockSpec((1, POOL_WIDTH), const2),
            pl.BlockSpec((ATT_WIDTH, D_MODEL), const2, pipeline_mode=once),
            pl.BlockSpec((POOL_WIDTH, D_MODEL), const2, pipeline_mode=once),
            pl.BlockSpec((D_MODEL, D_MODEL), const2, pipeline_mode=once),
            pl.BlockSpec((1, D_MODEL), const2),
        ],
        out_specs=[
            pl.BlockSpec((bm, D_MODEL), lambda i: (i, 0)),
            pl.BlockSpec((bm, D_MODEL), lambda i: (i, 0)),
        ],
        out_shape=[
            jax.ShapeDtypeStruct((S, D_MODEL), F32),
            jax.ShapeDtypeStruct((S, D_MODEL), BF16),
        ],
        scratch_shapes=[pltpu.VMEM((POOL_HALO + bm, POOL_WIDTH), F32)],
        compiler_params=_params(("arbitrary",)),
        name="mix_out",
    )(o, u, u, gates, x, wg, pool_scale, wa, wp, wo, g_mlp)


def _mlp_kernel(hm_ref, x1_ref, wu_ref, wd_ref, gp_ref, x2_ref, hp_ref, acc_ref):
    f = pl.program_id(1)

    @pl.when(f == 0)
    def _():
        acc_ref[...] = x1_ref[...]

    a = jnp.maximum(_dot(hm_ref[...], wu_ref[...]), 0.0)
    acc_ref[...] += _dot((a * a).astype(BF16), wd_ref[...])

    @pl.when(f == pl.num_programs(1) - 1)
    def _():
        x2 = acc_ref[...]
        x2_ref[...] = x2
        hp_ref[...] = _rms(x2, gp_ref[...]).astype(BF16)


def _mlp(hm, x1, wu, wd, g_ple, bm=512, bf=1024):
    S = hm.shape[0]
    return pl.pallas_call(
        _mlp_kernel,
        grid=(S // bm, D_FF // bf),
        in_specs=[
            pl.BlockSpec((bm, D_MODEL), lambda i, f: (i, 0)),
            pl.BlockSpec((bm, D_MODEL), lambda i, f: (i, 0)),
            pl.BlockSpec((D_MODEL, bf), lambda i, f: (0, f)),
            pl.BlockSpec((bf, D_MODEL), lambda i, f: (f, 0)),
            pl.BlockSpec((1, D_MODEL), lambda i, f: (0, 0)),
        ],
        out_specs=[
            pl.BlockSpec((bm, D_MODEL), lambda i, f: (i, 0)),
            pl.BlockSpec((bm, D_MODEL), lambda i, f: (i, 0)),
        ],
        out_shape=[
            jax.ShapeDtypeStruct((S, D_MODEL), F32),
            jax.ShapeDtypeStruct((S, D_MODEL), BF16),
        ],
        scratch_shapes=[pltpu.VMEM((bm, D_MODEL), F32)],
        compiler_params=_params(("arbitrary", "arbitrary")),
        name="mlp",
    )(hm, x1, wu, wd, g_ple)


def _ple_kernel(hp_ref, x2_ref, p_ref, wgate_ref, wple_ref, gf_ref, out_ref):
    gate = jax.nn.sigmoid(_dot(hp_ref[...], wgate_ref[...]))
    emb = _dot(p_ref[...].astype(BF16), wple_ref[...])
    x3 = x2_ref[...] + emb * gate
    out_ref[...] = _rms(x3, gf_ref[...])


def _ple(hp, x2, p, wgate, wple, g_final, bm=512):
    S = hp.shape[0]
    const2 = lambda i: (0, 0)
    return pl.pallas_call(
        _ple_kernel,
        grid=(S // bm,),
        in_specs=[
            pl.BlockSpec((bm, D_MODEL), lambda i: (i, 0)),
            pl.BlockSpec((bm, D_MODEL), lambda i: (i, 0)),
            pl.BlockSpec((bm, PLE_DIM), lambda i: (i, 0)),
            pl.BlockSpec((D_MODEL, D_MODEL), const2),
            pl.BlockSpec((PLE_DIM, D_MODEL), const2),
            pl.BlockSpec((1, D_MODEL), const2),
        ],
        out_specs=pl.BlockSpec((bm, D_MODEL), lambda i: (i, 0)),
        out_shape=jax.ShapeDtypeStruct((S, D_MODEL), F32),
        compiler_params=_params(("arbitrary",)),
        name="ple_final",
    )(hp, x2, p, wgate, wple, g_final)


def _head_major(w):
    return w.reshape(D_MODEL, 2, N_HEADS, QK_HEAD_DIM).transpose(0, 2, 1, 3).reshape(D_MODEL, 1024)


def kernel(x, p, norm_mix_g, w_in, lambda_q1, lambda_k1, lambda_q2, lambda_k2, subln_g, pool_grp_w, pool_scale, w_attn_br, w_pool_br, w_out, norm_mlp_g, w_mlp_up, w_mlp_down, norm_ple_g, w_ple, w_ple_gate, final_norm_g):
    B, S, D = x.shape
    assert (B, S, D) == (1, SEQ, D_MODEL) and norm_mix_g.shape[0] == 1
    slopes = jnp.asarray(2.0 ** (-8.0 * np.arange(1, N_HEADS + 1) / N_HEADS), dtype=F32)

    w = w_in[0]
    wq = _head_major(w[:, 0:1024]).astype(BF16)
    wk = _head_major(w[:, 1024:2048]).astype(BF16)
    wv = w[:, 2048:3072].astype(BF16)
    w_ug = w[:, 3072:].astype(BF16)

    x2d = x[0]
    h, qT, k, vT = _norm_qkv(x2d, norm_mix_g, wq, wk, wv)
    u, gates = _ug(h, w_ug)
    o = _attention(slopes, qT, k, vT, lambda_q1, lambda_k1, lambda_q2, lambda_k2, subln_g)
    x1, hm = _mix(o, u, gates, x2d, pool_grp_w[0].astype(BF16), pool_scale,
                  w_attn_br[0].astype(BF16), w_pool_br[0].astype(BF16), w_out[0].astype(BF16),
                  norm_mlp_g)
    x2, hp = _mlp(hm, x1, w_mlp_up[0].astype(BF16), w_mlp_down[0].astype(BF16), norm_ple_g)
    out = _ple(hp, x2, p[0, 0], w_ple_gate[0].astype(BF16), w_ple[0].astype(BF16),
               final_norm_g.reshape(1, D_MODEL))
    return out[None]
```

```python
import functools
import math

import numpy as np
import jax
import jax.numpy as jnp
from jax import lax
from jax.experimental import pallas as pl
from jax.experimental.pallas import tpu as pltpu

F32 = jnp.float32
BF16 = jnp.bfloat16

D_MODEL = 2048
SEQ = 8192
N_HEADS = 8
V_HEAD_DIM = 128
QK_HEAD_DIM = 64
ATT_WIDTH = N_HEADS * V_HEAD_DIM
POOL_WIDTH = 1024
POOL_WINDOWS = (2, 4, 8, 16)
POOL_GROUP_WIDTH = POOL_WIDTH // len(POOL_WINDOWS)
POOL_HALO = 16
D_FF = 4 * D_MODEL
PLE_DIM = 256
NORM_EPS = 1e-6
LAMBDA_INIT = 0.8 - 0.6 * math.exp(-0.3 * 0)
NEG_BIG = -1e30

ATT_TQ = 256
ATT_TK = 256
ATT_G = 2
VMEM_LIMIT = 56 * 1024 * 1024


def _rms(xf, g):
    return xf * lax.rsqrt(jnp.mean(xf * xf, axis=-1, keepdims=True) + NORM_EPS) * g


def _dot(a, b):
    return jnp.dot(a, b, preferred_element_type=F32)


def _params(sem, flags=None):
    return pltpu.CompilerParams(dimension_semantics=sem, vmem_limit_bytes=VMEM_LIMIT, flags=flags)


def _norm_qkv_kernel(x_ref, g_ref, wq_ref, wk_ref, wv_ref, h_ref, qT_ref, k_ref, vT_ref, *, bm):
    h = _rms(x_ref[...], g_ref[...]).astype(BF16)
    h_ref[...] = h
    q = _dot(h, wq_ref[...]) * (QK_HEAD_DIM ** -0.5)
    for hh in range(N_HEADS):
        for r in range(bm // ATT_TQ):
            blk = q[r * ATT_TQ:(r + 1) * ATT_TQ, hh * 128:(hh + 1) * 128]
            qT_ref[hh, r] = blk.T.astype(BF16)
    k_ref[...] = _dot(h, wk_ref[...]).astype(BF16)
    v = _dot(h, wv_ref[...])
    for hh in range(N_HEADS):
        for r in range(bm // ATT_TK):
            blk = v[r * ATT_TK:(r + 1) * ATT_TK, hh * 128:(hh + 1) * 128]
            vT_ref[hh, r] = blk.T.astype(BF16)


def _norm_qkv(x, g, wq, wk, wv, bm=512):
    S = x.shape[0]
    nq, nk = S // ATT_TQ, S // ATT_TK
    const = lambda i: (0, 0)
    return pl.pallas_call(
        functools.partial(_norm_qkv_kernel, bm=bm),
        grid=(S // bm,),
        in_specs=[
            pl.BlockSpec((bm, D_MODEL), lambda i: (i, 0)),
            pl.BlockSpec((1, D_MODEL), const),
            pl.BlockSpec((D_MODEL, 1024), const),
            pl.BlockSpec((D_MODEL, 1024), const),
            pl.BlockSpec((D_MODEL, 1024), const),
        ],
        out_specs=[
            pl.BlockSpec((bm, D_MODEL), lambda i: (i, 0)),
            pl.BlockSpec((N_HEADS, bm // ATT_TQ, 128, ATT_TQ), lambda i: (0, i, 0, 0)),
            pl.BlockSpec((bm, 1024), lambda i: (i, 0)),
            pl.BlockSpec((N_HEADS, bm // ATT_TK, 128, ATT_TK), lambda i: (0, i, 0, 0)),
        ],
        out_shape=[
            jax.ShapeDtypeStruct((S, D_MODEL), BF16),
            jax.ShapeDtypeStruct((N_HEADS, nq, 128, ATT_TQ), BF16),
            jax.ShapeDtypeStruct((S, 1024), BF16),
            jax.ShapeDtypeStruct((N_HEADS, nk, 128, ATT_TK), BF16),
        ],
        compiler_params=_params(("arbitrary",)),
        name="norm_qkv",
    )(x, g, wq, wk, wv)


def _ug_kernel(h_ref, w_ref, u_ref, gate_ref):
    j = pl.program_id(1)
    z = _dot(h_ref[...], w_ref[...])

    @pl.when(j == 0)
    def _():
        u_ref[...] = z

    @pl.when(j > 0)
    def _():
        gate_ref[...] = jax.nn.sigmoid(z).astype(BF16)


def _ug(h, w_ug, bm=1024, bn=1024):
    S = h.shape[0]
    n = w_ug.shape[1]
    return pl.pallas_call(
        _ug_kernel,
        grid=(S // bm, n // bn),
        in_specs=[
            pl.BlockSpec((bm, D_MODEL), lambda i, j: (i, 0)),
            pl.BlockSpec((D_MODEL, bn), lambda i, j: (0, j)),
        ],
        out_specs=[
            pl.BlockSpec((bm, POOL_WIDTH), lambda i, j: (i, 0)),
            pl.BlockSpec((bm, bn), lambda i, j: (i, jnp.maximum(j - 1, 0))),
        ],
        out_shape=[
            jax.ShapeDtypeStruct((S, POOL_WIDTH), F32),
            jax.ShapeDtypeStruct((S, 2 * D_MODEL), BF16),
        ],
        compiler_params=_params(("arbitrary", "arbitrary")),
        name="norm_ug",
    )(h, w_ug)


def _attn_kernel(slope_ref, qT_ref, k_ref, vT_ref, lq1_ref, lk1_ref, lq2_ref, lk2_ref, g_ref,
                 o_ref, qs_ref, acc_ref, p_ref):
    hg = pl.program_id(0)
    qi = pl.program_id(1)
    tq, tk = ATT_TQ, ATT_TK

    row = lax.broadcasted_iota(jnp.int32, (128, tq), 0)
    for g in range(ATT_G):
        qT = qT_ref[g]
        zero = jnp.zeros_like(qT)
        qs_ref[g, :, :tq] = jnp.where(row < QK_HEAD_DIM, qT, zero)
        qs_ref[g, :, tq:] = jnp.where(row >= QK_HEAD_DIM, qT, zero)
    acc_ref[...] = jnp.zeros_like(acc_ref)

    key_local = lax.broadcasted_iota(jnp.int32, (tk, 1), 0).astype(F32)
    slopes = [slope_ref[hg * ATT_G + g] for g in range(ATT_G)]

    p_ref[...] = jnp.zeros_like(p_ref)

    def pv(j, alphas):
        for g in range(ATT_G):
            acc_ref[g] = alphas[g] * acc_ref[g] + _dot(vT_ref[g, j], p_ref[g])

    def steps(j, ms, ls, alphas, masked):
        start = pl.multiple_of(j * tk, tk)
        sTs = []
        for g in range(ATT_G):
            k_blk = k_ref[pl.ds(start, tk), g * 128:(g + 1) * 128]
            sT = _dot(k_blk, qs_ref[g]) + slopes[g] * key_local
            if masked:
                kpos = lax.broadcasted_iota(jnp.int32, (tk, 2 * tq), 0)
                qcol = lax.broadcasted_iota(jnp.int32, (tk, 2 * tq), 1)
                qpos = jnp.where(qcol >= tq, qcol - tq, qcol)
                sT = jnp.where(kpos <= qpos, sT, NEG_BIG)
            sTs.append(sT)
        pv(jnp.maximum(j - 1, 0), alphas)
        new_ms, new_ls, new_alphas = [], [], []
        for g in range(ATT_G):
            off = slopes[g] * (j * tk - qi * tq).astype(F32)
            m_new = jnp.maximum(ms[g], jnp.max(sTs[g], axis=0, keepdims=True) + off)
            alpha = jnp.exp(ms[g] - m_new)
            pT = jnp.exp(sTs[g] - (m_new - off))
            new_ls.append(alpha * ls[g] + jnp.sum(pT, axis=0, keepdims=True))
            new_ms.append(m_new)
            new_alphas.append(alpha)
            p_ref[g] = pT.astype(BF16)
        return tuple(new_ms), tuple(new_ls), tuple(new_alphas)

    def body(j, carry):
        return steps(j, *carry, False)

    init = (tuple(jnp.full((1, 2 * tq), NEG_BIG, F32) for _ in range(ATT_G)),
            tuple(jnp.zeros((1, 2 * tq), F32) for _ in range(ATT_G)),
            tuple(jnp.ones((1, 2 * tq), F32) for _ in range(ATT_G)))
    carry = lax.fori_loop(0, qi, body, init)
    ms, ls, alphas = steps(qi, *carry, True)
    pv(qi, alphas)

    lam = (jnp.exp(jnp.sum(lq1_ref[...] * lk1_ref[...], axis=-1, keepdims=True))
           - jnp.exp(jnp.sum(lq2_ref[...] * lk2_ref[...], axis=-1, keepdims=True))
           + LAMBDA_INIT)
    for g in range(ATT_G):
        acc = acc_ref[g]
        l = ls[g]
        oT = acc[:, :tq] / l[:, :tq] - lam * (acc[:, tq:] / l[:, tq:])
        o = _rms(oT.T, g_ref[...]) * (1.0 - LAMBDA_INIT)
        o_ref[:, g * 128:(g + 1) * 128] = o.astype(BF16)


def _attention(slopes, qT, k, vT, lq1, lk1, lq2, lk2, subln_g):
    S = k.shape[0]
    nq, nk = S // ATT_TQ, S // ATT_TK
    G = ATT_G
    vec = lambda h, i: (0, 0)
    return pl.pallas_call(
        _attn_kernel,
        grid=(N_HEADS // G, nq),
        in_specs=[
            pl.BlockSpec(memory_space=pltpu.SMEM),
            pl.BlockSpec((G, None, 128, ATT_TQ), lambda h, i: (h, i, 0, 0)),
            pl.BlockSpec((S, G * 128), lambda h, i: (0, h)),
            pl.BlockSpec((G, nk, 128, ATT_TK), lambda h, i: (h, 0, 0, 0)),
            pl.BlockSpec((1, QK_HEAD_DIM), vec),
            pl.BlockSpec((1, QK_HEAD_DIM), vec),
            pl.BlockSpec((1, QK_HEAD_DIM), vec),
            pl.BlockSpec((1, QK_HEAD_DIM), vec),
            pl.BlockSpec((1, V_HEAD_DIM), vec),
        ],
        out_specs=pl.BlockSpec((ATT_TQ, G * 128), lambda h, i: (i, h)),
        out_shape=jax.ShapeDtypeStruct((S, ATT_WIDTH), BF16),
        scratch_shapes=[
            pltpu.VMEM((G, 128, 2 * ATT_TQ), BF16),
            pltpu.VMEM((G, 128, 2 * ATT_TQ), F32),
            pltpu.VMEM((G, ATT_TK, 2 * ATT_TQ), BF16),
        ],
        compiler_params=_params(("arbitrary", "arbitrary")),
        name="diff_attn",
    )(slopes, qT, k, vT, lq1, lk1, lq2, lk2, subln_g)


def _mix_kernel(o_ref, u_ref, uprev_ref, gate_ref, x_ref, wg_ref, ps_ref, wa_ref, wp_ref, wo_ref,
                gm_ref, x1_ref, hm_ref, ext_ref, *, bm):
    i = pl.program_id(0)
    u = u_ref[...]
    halo = uprev_ref[...]
    ext_ref[0:POOL_HALO, :] = jnp.where(i == 0, jnp.zeros_like(halo), halo)
    ext_ref[POOL_HALO:POOL_HALO + bm, :] = u
    t = i * bm + lax.broadcasted_iota(jnp.int32, (bm, 1), 0)
    pms = []
    for g, w in enumerate(POOL_WINDOWS):
        c0, c1 = g * POOL_GROUP_WIDTH, (g + 1) * POOL_GROUP_WIDTH
        ug = u[:, c0:c1]
        win = ug
        for d in range(1, w):
            win = win + ext_ref[POOL_HALO - d:POOL_HALO - d + bm, c0:c1]
        count = jnp.minimum(t + 1, w).astype(F32)
        pooled = win / count - ug
        pms.append(_dot(pooled.astype(BF16), wg_ref[g]))
    pm = (jnp.concatenate(pms, axis=-1) * ps_ref[...]).astype(BF16)
    p_branch = _dot(pm, wp_ref[...])
    a_branch = _dot(o_ref[...], wa_ref[...])
    gates = gate_ref[...]
    merged = gates[:, :D_MODEL].astype(F32) * a_branch + gates[:, D_MODEL:].astype(F32) * p_branch
    x1 = x_ref[...] + _dot(merged.astype(BF16), wo_ref[...])
    x1_ref[...] = x1
    hm_ref[...] = _rms(x1, gm_ref[...]).astype(BF16)


def _mix(o, u, gates, x, wg, pool_scale, wa, wp, wo, g_mlp, bm=256):
    S = x.shape[0]
    const2 = lambda i: (0, 0)
    once = pl.Buffered(1)
    halo_blocks = bm // POOL_HALO
    return pl.pallas_call(
        functools.partial(_mix_kernel, bm=bm),
        grid=(S // bm,),
        in_specs=[
            pl.BlockSpec((bm, ATT_WIDTH), lambda i: (i, 0)),
            pl.BlockSpec((bm, POOL_WIDTH), lambda i: (i, 0)),
            pl.BlockSpec((POOL_HALO, POOL_WIDTH), lambda i: (jnp.maximum(i * halo_blocks - 1, 0), 0)),
            pl.BlockSpec((bm, 2 * D_MODEL), lambda i: (i, 0)),
            pl.BlockSpec((bm, D_MODEL), lambda i: (i, 0)),
            pl.BlockSpec((len(POOL_WINDOWS), POOL_GROUP_WIDTH, POOL_GROUP_WIDTH), lambda i: (0, 0, 0),
                         pipeline_mode=once),
            pl.BlockSpec((1, POOL_WIDTH), const2),
            pl.BlockSpec((ATT_WIDTH, D_MODEL), const2, pipeline_mode=once),
            pl.BlockSpec((POOL_WIDTH, D_MODEL), const2, pipeline_mode=once),
            pl.BlockSpec((D_MODEL, D_MODEL), const2, pipeline_mode=once),
            pl.BlockSpec((1, D_MODEL), const2),
        ],
        out_specs=[
            pl.BlockSpec((bm, D_MODEL), lambda i: (i, 0)),
            pl.BlockSpec((bm, D_MODEL), lambda i: (i, 0)),
        ],
        out_shape=[
            jax.ShapeDtypeStruct((S, D_MODEL), F32),
            jax.ShapeDtypeStruct((S, D_MODEL), BF16),
        ],
        scratch_shapes=[pltpu.VMEM((POOL_HALO + bm, POOL_WIDTH), F32)],
        compiler_params=_params(("arbitrary",)),
        name="mix_out",
    )(o, u, u, gates, x, wg, pool_scale, wa, wp, wo, g_mlp)


def _mlp_kernel(hm_ref, x1_ref, wu_ref, wd_ref, gp_ref, x2_ref, hp_ref, acc_ref):
    f = pl.program_id(1)

    @pl.when(f == 0)
    def _():
        acc_ref[...] = x1_ref[...]

    a = jnp.maximum(_dot(hm_ref[...], wu_ref[...]), 0.0)
    acc_ref[...] += _dot((a * a).astype(BF16), wd_ref[...])

    @pl.when(f == pl.num_programs(1) - 1)
    def _():
        x2 = acc_ref[...]
        x2_ref[...] = x2
        hp_ref[...] = _rms(x2, gp_ref[...]).astype(BF16)


def _mlp(hm, x1, wu, wd, g_ple, bm=512, bf=1024):
    S = hm.shape[0]
    return pl.pallas_call(
        _mlp_kernel,
        grid=(S // bm, D_FF // bf),
        in_specs=[
            pl.BlockSpec((bm, D_MODEL), lambda i, f: (i, 0)),
            pl.BlockSpec((bm, D_MODEL), lambda i, f: (i, 0)),
            pl.BlockSpec((D_MODEL, bf), lambda i, f: (0, f)),
            pl.BlockSpec((bf, D_MODEL), lambda i, f: (f, 0)),
            pl.BlockSpec((1, D_MODEL), lambda i, f: (0, 0)),
        ],
        out_specs=[
            pl.BlockSpec((bm, D_MODEL), lambda i, f: (i, 0)),
            pl.BlockSpec((bm, D_MODEL), lambda i, f: (i, 0)),
        ],
        out_shape=[
            jax.ShapeDtypeStruct((S, D_MODEL), F32),
            jax.ShapeDtypeStruct((S, D_MODEL), BF16),
        ],
        scratch_shapes=[pltpu.VMEM((bm, D_MODEL), F32)],
        compiler_params=_params(("arbitrary", "arbitrary")),
        name="mlp",
    )(hm, x1, wu, wd, g_ple)


def _ple_kernel(hp_ref, x2_ref, p_ref, wgate_ref, wple_ref, gf_ref, out_ref):
    gate = jax.nn.sigmoid(_dot(hp_ref[...], wgate_ref[...]))
    emb = _dot(p_ref[...].astype(BF16), wple_ref[...])
    x3 = x2_ref[...] + emb * gate
    out_ref[...] = _rms(x3, gf_ref[...])


def _ple(hp, x2, p, wgate, wple, g_final, bm=512):
    S = hp.shape[0]
    const2 = lambda i: (0, 0)
    return pl.pallas_call(
        _ple_kernel,
        grid=(S // bm,),
        in_specs=[
            pl.BlockSpec((bm, D_MODEL), lambda i: (i, 0)),
            pl.BlockSpec((bm, D_MODEL), lambda i: (i, 0)),
            pl.BlockSpec((bm, PLE_DIM), lambda i: (i, 0)),
            pl.BlockSpec((D_MODEL, D_MODEL), const2),
            pl.BlockSpec((PLE_DIM, D_MODEL), const2),
            pl.BlockSpec((1, D_MODEL), const2),
        ],
        out_specs=pl.BlockSpec((bm, D_MODEL), lambda i: (i, 0)),
        out_shape=jax.ShapeDtypeStruct((S, D_MODEL), F32),
        compiler_params=_params(("arbitrary",)),
        name="ple_final",
    )(hp, x2, p, wgate, wple, g_final)


def _head_major(w):
    return w.reshape(D_MODEL, 2, N_HEADS, QK_HEAD_DIM).transpose(0, 2, 1, 3).reshape(D_MODEL, 1024)


def kernel(x, p, norm_mix_g, w_in, lambda_q1, lambda_k1, lambda_q2, lambda_k2, subln_g, pool_grp_w, pool_scale, w_attn_br, w_pool_br, w_out, norm_mlp_g, w_mlp_up, w_mlp_down, norm_ple_g, w_ple, w_ple_gate, final_norm_g):
    B, S, D = x.shape
    assert (B, S, D) == (1, SEQ, D_MODEL) and norm_mix_g.shape[0] == 1
    slopes = jnp.asarray(2.0 ** (-8.0 * np.arange(1, N_HEADS + 1) / N_HEADS), dtype=F32)

    w = w_in[0]
    wq = _head_major(w[:, 0:1024]).astype(BF16)
    wk = _head_major(w[:, 1024:2048]).astype(BF16)
    wv = w[:, 2048:3072].astype(BF16)
    w_ug = w[:, 3072:].astype(BF16)

    x2d = x[0]
    h, qT, k, vT = _norm_qkv(x2d, norm_mix_g, wq, wk, wv)
    u, gates = _ug(h, w_ug)
    o = _attention(slopes, qT, k, vT, lambda_q1, lambda_k1, lambda_q2, lambda_k2, subln_g)
    x1, hm = _mix(o, u, gates, x2d, pool_grp_w[0].astype(BF16), pool_scale,
                  w_attn_br[0].astype(BF16), w_pool_br[0].astype(BF16), w_out[0].astype(BF16),
                  norm_mlp_g)
    x2, hp = _mlp(hm, x1, w_mlp_up[0].astype(BF16), w_mlp_down[0].astype(BF16), norm_ple_g)
    out = _ple(hp, x2, p[0, 0], w_ple_gate[0].astype(BF16), w_ple[0].astype(BF16),
               final_norm_g.reshape(1, D_MODEL))
    return out[None]
```

```python
import functools
import math

import numpy as np
import jax
import jax.numpy as jnp
from jax import lax
from jax.experimental import pallas as pl
from jax.experimental.pallas import tpu as pltpu

F32 = jnp.float32
BF16 = jnp.bfloat16

D_MODEL = 2048
SEQ = 8192
N_HEADS = 8
V_HEAD_DIM = 128
QK_HEAD_DIM = 64
ATT_WIDTH = N_HEADS * V_HEAD_DIM
POOL_WIDTH = 1024
POOL_WINDOWS = (2, 4, 8, 16)
POOL_GROUP_WIDTH = POOL_WIDTH // len(POOL_WINDOWS)
POOL_HALO = 16
D_FF = 4 * D_MODEL
PLE_DIM = 256
NORM_EPS = 1e-6
LAMBDA_INIT = 0.8 - 0.6 * math.exp(-0.3 * 0)
NEG_BIG = -1e30
LOG2E = math.log2(math.e)
V_ROWS = V_HEAD_DIM + 16
BIAS_TERMS = 3


def _alibi_parts():
    import ml_dtypes
    rest = 2.0 ** (-8.0 * np.arange(1, N_HEADS + 1) / N_HEADS) * LOG2E
    parts = []
    for _ in range(BIAS_TERMS):
        piece = rest.astype(ml_dtypes.bfloat16).astype(np.float64)
        parts.append(piece)
        rest = rest - piece
    return np.stack(parts, axis=1).reshape(-1).astype(np.float32)

ATT_TQ = 256
ATT_TK = 256
ATT_G = 2
VMEM_LIMIT = 56 * 1024 * 1024


def _rms(xf, g):
    return xf * lax.rsqrt(jnp.mean(xf * xf, axis=-1, keepdims=True) + NORM_EPS) * g


def _dot(a, b):
    return jnp.dot(a, b, preferred_element_type=F32)


def _params(sem, flags=None):
    return pltpu.CompilerParams(dimension_semantics=sem, vmem_limit_bytes=VMEM_LIMIT, flags=flags)


def _norm_qkv_kernel(x_ref, g_ref, wq_ref, wk_ref, wv_ref, h_ref, qT_ref, k_ref, vT_ref, *, bm):
    h = _rms(x_ref[...], g_ref[...]).astype(BF16)
    h_ref[...] = h
    q = _dot(h, wq_ref[...]) * (QK_HEAD_DIM ** -0.5 * LOG2E)
    for hh in range(N_HEADS):
        for r in range(bm // ATT_TQ):
            blk = q[r * ATT_TQ:(r + 1) * ATT_TQ, hh * 128:(hh + 1) * 128]
            qT_ref[hh, r] = blk.T.astype(BF16)
    k_ref[...] = _dot(h, wk_ref[...]).astype(BF16)
    v = _dot(h, wv_ref[...])
    ones = jnp.ones((V_ROWS - V_HEAD_DIM, ATT_TK), BF16)
    for hh in range(N_HEADS):
        for r in range(bm // ATT_TK):
            blk = v[r * ATT_TK:(r + 1) * ATT_TK, hh * 128:(hh + 1) * 128]
            vT_ref[hh, r, 0:V_HEAD_DIM, :] = blk.T.astype(BF16)
            vT_ref[hh, r, V_HEAD_DIM:V_ROWS, :] = ones


def _norm_qkv(x, g, wq, wk, wv, bm=512):
    S = x.shape[0]
    nq, nk = S // ATT_TQ, S // ATT_TK
    const = lambda i: (0, 0)
    return pl.pallas_call(
        functools.partial(_norm_qkv_kernel, bm=bm),
        grid=(S // bm,),
        in_specs=[
            pl.BlockSpec((bm, D_MODEL), lambda i: (i, 0)),
            pl.BlockSpec((1, D_MODEL), const),
            pl.BlockSpec((D_MODEL, 1024), const),
            pl.BlockSpec((D_MODEL, 1024), const),
            pl.BlockSpec((D_MODEL, 1024), const),
        ],
        out_specs=[
            pl.BlockSpec((bm, D_MODEL), lambda i: (i, 0)),
            pl.BlockSpec((N_HEADS, bm // ATT_TQ, 128, ATT_TQ), lambda i: (0, i, 0, 0)),
            pl.BlockSpec((bm, 1024), lambda i: (i, 0)),
            pl.BlockSpec((N_HEADS, bm // ATT_TK, V_ROWS, ATT_TK), lambda i: (0, i, 0, 0)),
        ],
        out_shape=[
            jax.ShapeDtypeStruct((S, D_MODEL), BF16),
            jax.ShapeDtypeStruct((N_HEADS, nq, 128, ATT_TQ), BF16),
            jax.ShapeDtypeStruct((S, 1024), BF16),
            jax.ShapeDtypeStruct((N_HEADS, nk, V_ROWS, ATT_TK), BF16),
        ],
        compiler_params=_params(("arbitrary",)),
        name="norm_qkv",
    )(x, g, wq, wk, wv)


def _ug_kernel(h_ref, w_ref, u_ref, gate_ref):
    j = pl.program_id(1)
    z = _dot(h_ref[...], w_ref[...])

    @pl.when(j == 0)
    def _():
        u_ref[...] = z

    @pl.when(j > 0)
    def _():
        gate_ref[...] = jax.nn.sigmoid(z).astype(BF16)


def _ug(h, w_ug, bm=1024, bn=1024):
    S = h.shape[0]
    n = w_ug.shape[1]
    return pl.pallas_call(
        _ug_kernel,
        grid=(S // bm, n // bn),
        in_specs=[
            pl.BlockSpec((bm, D_MODEL), lambda i, j: (i, 0)),
            pl.BlockSpec((D_MODEL, bn), lambda i, j: (0, j)),
        ],
        out_specs=[
            pl.BlockSpec((bm, POOL_WIDTH), lambda i, j: (i, 0)),
            pl.BlockSpec((bm, bn), lambda i, j: (i, jnp.maximum(j - 1, 0))),
        ],
        out_shape=[
            jax.ShapeDtypeStruct((S, POOL_WIDTH), F32),
            jax.ShapeDtypeStruct((S, 2 * D_MODEL), BF16),
        ],
        compiler_params=_params(("arbitrary", "arbitrary")),
        name="norm_ug",
    )(h, w_ug)


def _attn_kernel(cpart_ref, qT_ref, k_ref, vT_ref, lq1_ref, lk1_ref, lq2_ref, lk2_ref, g_ref,
                 o_ref, qs_ref, acc_ref, p_ref, feat_ref):
    hg = pl.program_id(0)
    qi = pl.program_id(1)
    tq, tk = ATT_TQ, ATT_TK

    krow = lax.broadcasted_iota(jnp.int32, (tk, 128), 0)
    klane = lax.broadcasted_iota(jnp.int32, (tk, 128), 1)
    feat = jnp.where(klane < BIAS_TERMS, krow >> 4, jnp.where(klane < 2 * BIAS_TERMS, krow & 15, 0))
    feat_ref[...] = feat.astype(F32).astype(BF16)

    row = lax.broadcasted_iota(jnp.int32, (128, tq), 0)
    brow = lax.broadcasted_iota(jnp.int32, (128, 2 * tq), 0)
    slopes = []
    for g in range(ATT_G):
        qT = qT_ref[g]
        zero = jnp.zeros_like(qT)
        qs_ref[g, 0:128, :tq] = jnp.where(row < QK_HEAD_DIM, qT, zero)
        qs_ref[g, 0:128, tq:] = jnp.where(row >= QK_HEAD_DIM, qT, zero)
        parts = [cpart_ref[(hg * ATT_G + g) * BIAS_TERMS + t] for t in range(BIAS_TERMS)]
        bias_rows = jnp.zeros((128, 2 * tq), F32)
        for t in range(BIAS_TERMS):
            bias_rows = jnp.where(brow == t, 16.0 * parts[t], bias_rows)
            bias_rows = jnp.where(brow == BIAS_TERMS + t, parts[t], bias_rows)
        qs_ref[g, 128:256, :] = bias_rows.astype(BF16)
        slopes.append(sum(parts[1:], parts[0]))
    acc_ref[...] = jnp.zeros_like(acc_ref)
    p_ref[...] = jnp.zeros_like(p_ref)

    def pv(j, alphas):
        for g in range(ATT_G):
            acc_ref[g] = alphas[g] * acc_ref[g] + _dot(vT_ref[g, j], p_ref[g])

    def steps(j, ms, alphas, masked):
        start = pl.multiple_of(j * tk, tk)
        sTs = []
        for g in range(ATT_G):
            k_blk = k_ref[pl.ds(start, tk), g * 128:(g + 1) * 128]
            k_aug = jnp.concatenate([k_blk, feat_ref[...]], axis=1)
            sT = _dot(k_aug, qs_ref[g])
            if masked:
                kpos = lax.broadcasted_iota(jnp.int32, (tk, 2 * tq), 0)
                qcol = lax.broadcasted_iota(jnp.int32, (tk, 2 * tq), 1)
                qpos = jnp.where(qcol >= tq, qcol - tq, qcol)
                sT = jnp.where(kpos <= qpos, sT, NEG_BIG)
            sTs.append(sT)
        pv(jnp.maximum(j - 1, 0), alphas)
        new_ms, new_alphas = [], []
        for g in range(ATT_G):
            off = slopes[g] * (j * tk - qi * tq).astype(F32)
            m_new = jnp.maximum(ms[g], jnp.max(sTs[g], axis=0, keepdims=True) + off)
            new_alphas.append(jnp.exp2(ms[g] - m_new))
            new_ms.append(m_new)
            p_ref[g] = jnp.exp2(sTs[g] - (m_new - off)).astype(BF16)
        return tuple(new_ms), tuple(new_alphas)

    def body(j, carry):
        return steps(j, *carry, False)

    init = (tuple(jnp.full((1, 2 * tq), NEG_BIG, F32) for _ in range(ATT_G)),
            tuple(jnp.ones((1, 2 * tq), F32) for _ in range(ATT_G)))
    carry = lax.fori_loop(0, qi, body, init)
    ms, alphas = steps(qi, *carry, True)
    pv(qi, alphas)

    lam = (jnp.exp(jnp.sum(lq1_ref[...] * lk1_ref[...], axis=-1, keepdims=True))
           - jnp.exp(jnp.sum(lq2_ref[...] * lk2_ref[...], axis=-1, keepdims=True))
           + LAMBDA_INIT)
    for g in range(ATT_G):
        acc = acc_ref[g]
        inv_l = 1.0 / acc[V_HEAD_DIM:V_HEAD_DIM + 1, :]
        num = acc[:V_HEAD_DIM, :] * inv_l
        oT = num[:, :tq] - lam * num[:, tq:]
        o = _rms(oT.T, g_ref[...]) * (1.0 - LAMBDA_INIT)
        o_ref[:, g * 128:(g + 1) * 128] = o.astype(BF16)


def _attention(cparts, qT, k, vT, lq1, lk1, lq2, lk2, subln_g):
    S = k.shape[0]
    nq, nk = S // ATT_TQ, S // ATT_TK
    G = ATT_G
    vec = lambda h, i: (0, 0)
    return pl.pallas_call(
        _attn_kernel,
        grid=(N_HEADS // G, nq),
        in_specs=[
            pl.BlockSpec(memory_space=pltpu.SMEM),
            pl.BlockSpec((G, None, 128, ATT_TQ), lambda h, i: (h, i, 0, 0)),
            pl.BlockSpec((S, G * 128), lambda h, i: (0, h)),
            pl.BlockSpec((G, nk, V_ROWS, ATT_TK), lambda h, i: (h, 0, 0, 0)),
            pl.BlockSpec((1, QK_HEAD_DIM), vec),
            pl.BlockSpec((1, QK_HEAD_DIM), vec),
            pl.BlockSpec((1, QK_HEAD_DIM), vec),
            pl.BlockSpec((1, QK_HEAD_DIM), vec),
            pl.BlockSpec((1, V_HEAD_DIM), vec),
        ],
        out_specs=pl.BlockSpec((ATT_TQ, G * 128), lambda h, i: (i, h)),
        out_shape=jax.ShapeDtypeStruct((S, ATT_WIDTH), BF16),
        scratch_shapes=[
            pltpu.VMEM((G, 256, 2 * ATT_TQ), BF16),
            pltpu.VMEM((G, V_ROWS, 2 * ATT_TQ), F32),
            pltpu.VMEM((G, ATT_TK, 2 * ATT_TQ), BF16),
            pltpu.VMEM((ATT_TK, 128), BF16),
        ],
        compiler_params=_params(("arbitrary", "arbitrary")),
        name="diff_attn",
    )(cparts, qT, k, vT, lq1, lk1, lq2, lk2, subln_g)


def _mix_kernel(o_ref, u_ref, uprev_ref, gate_ref, x_ref, wg_ref, ps_ref, wa_ref, wp_ref, wo_ref,
                gm_ref, x1_ref, hm_ref, ext_ref, *, bm):
    i = pl.program_id(0)
    u = u_ref[...]
    halo = uprev_ref[...]
    ext_ref[0:POOL_HALO, :] = jnp.where(i == 0, jnp.zeros_like(halo), halo)
    ext_ref[POOL_HALO:POOL_HALO + bm, :] = u
    t = i * bm + lax.broadcasted_iota(jnp.int32, (bm, 1), 0)
    pms = []
    for g, w in enumerate(POOL_WINDOWS):
        c0, c1 = g * POOL_GROUP_WIDTH, (g + 1) * POOL_GROUP_WIDTH
        ug = u[:, c0:c1]
        win = ug
        for d in range(1, w):
            win = win + ext_ref[POOL_HALO - d:POOL_HALO - d + bm, c0:c1]
        count = jnp.minimum(t + 1, w).astype(F32)
        pooled = win / count - ug
        pms.append(_dot(pooled.astype(BF16), wg_ref[g]))
    pm = (jnp.concatenate(pms, axis=-1) * ps_ref[...]).astype(BF16)
    p_branch = _dot(pm, wp_ref[...])
    a_branch = _dot(o_ref[...], wa_ref[...])
    gates = gate_ref[...]
    merged = gates[:, :D_MODEL].astype(F32) * a_branch + gates[:, D_MODEL:].astype(F32) * p_branch
    x1 = x_ref[...] + _dot(merged.astype(BF16), wo_ref[...])
    x1_ref[...] = x1
    hm_ref[...] = _rms(x1, gm_ref[...]).astype(BF16)


def _mix(o, u, gates, x, wg, pool_scale, wa, wp, wo, g_mlp, bm=256):
    S = x.shape[0]
    const2 = lambda i: (0, 0)
    once = pl.Buffered(1)
    halo_blocks = bm // POOL_HALO
    return pl.pallas_call(
        functools.partial(_mix_kernel, bm=bm),
        grid=(S // bm,),
        in_specs=[
            pl.BlockSpec((bm, ATT_WIDTH), lambda i: (i, 0)),
            pl.BlockSpec((bm, POOL_WIDTH), lambda i: (i, 0)),
            pl.BlockSpec((POOL_HALO, POOL_WIDTH), lambda i: (jnp.maximum(i * halo_blocks - 1, 0), 0)),
            pl.BlockSpec((bm, 2 * D_MODEL), lambda i: (i, 0)),
            pl.BlockSpec((bm, D_MODEL), lambda i: (i, 0)),
            pl.BlockSpec((len(POOL_WINDOWS), POOL_GROUP_WIDTH, POOL_GROUP_WIDTH), lambda i: (0, 0, 0),
                         pipeline_mode=once),
            pl.BlockSpec((1, POOL_WIDTH), const2),
            pl.BlockSpec((ATT_WIDTH, D_MODEL), const2, pipeline_mode=once),
            pl.BlockSpec((POOL_WIDTH, D_MODEL), const2, pipeline_mode=once),
            pl.BlockSpec((D_MODEL, D_MODEL), const2, pipeline_mode=once),
            pl.BlockSpec((1, D_MODEL), const2),
        ],
        out_specs=[
            pl.BlockSpec((bm, D_MODEL), lambda i: (i, 0)),
            pl.BlockSpec((bm, D_MODEL), lambda i: (i, 0)),
        ],
        out_shape=[
            jax.ShapeDtypeStruct((S, D_MODEL), F32),
            jax.ShapeDtypeStruct((S, D_MODEL), BF16),
        ],
        scratch_shapes=[pltpu.VMEM((POOL_HALO + bm, POOL_WIDTH), F32)],
        compiler_params=_params(("arbitrary",)),
        name="mix_out",
    )(o, u, u, gates, x, wg, pool_scale, wa, wp, wo, g_mlp)


def _mlp_kernel(hm_ref, x1_ref, wu_ref, wd_ref, gp_ref, x2_ref, hp_ref, acc_ref):
    f = pl.program_id(1)

    @pl.when(f == 0)
    def _():
        acc_ref[...] = x1_ref[...]

    a = jnp.maximum(_dot(hm_ref[...], wu_ref[...]), 0.0)
    acc_ref[...] += _dot((a * a).astype(BF16), wd_ref[...])

    @pl.when(f == pl.num_programs(1) - 1)
    def _():
        x2 = acc_ref[...]
        x2_ref[...] = x2
        hp_ref[...] = _rms(x2, gp_ref[...]).astype(BF16)


def _mlp(hm, x1, wu, wd, g_ple, bm=512, bf=1024):
    S = hm.shape[0]
    return pl.pallas_call(
        _mlp_kernel,
        grid=(S // bm, D_FF // bf),
        in_specs=[
            pl.BlockSpec((bm, D_MODEL), lambda i, f: (i, 0)),
            pl.BlockSpec((bm, D_MODEL), lambda i, f: (i, 0)),
            pl.BlockSpec((D_MODEL, bf), lambda i, f: (0, f)),
            pl.BlockSpec((bf, D_MODEL), lambda i, f: (f, 0)),
            pl.BlockSpec((1, D_MODEL), lambda i, f: (0, 0)),
        ],
        out_specs=[
            pl.BlockSpec((bm, D_MODEL), lambda i, f: (i, 0)),
            pl.BlockSpec((bm, D_MODEL), lambda i, f: (i, 0)),
        ],
        out_shape=[
            jax.ShapeDtypeStruct((S, D_MODEL), F32),
            jax.ShapeDtypeStruct((S, D_MODEL), BF16),
        ],
        scratch_shapes=[pltpu.VMEM((bm, D_MODEL), F32)],
        compiler_params=_params(("arbitrary", "arbitrary")),
        name="mlp",
    )(hm, x1, wu, wd, g_ple)


def _ple_kernel(hp_ref, x2_ref, p_ref, wgate_ref, wple_ref, gf_ref, out_ref):
    gate = jax.nn.sigmoid(_dot(hp_ref[...], wgate_ref[...]))
    emb = _dot(p_ref[...].astype(BF16), wple_ref[...])
    x3 = x2_ref[...] + emb * gate
    out_ref[...] = _rms(x3, gf_ref[...])


def _ple(hp, x2, p, wgate, wple, g_final, bm=512):
    S = hp.shape[0]
    const2 = lambda i: (0, 0)
    return pl.pallas_call(
        _ple_kernel,
        grid=(S // bm,),
        in_specs=[
            pl.BlockSpec((bm, D_MODEL), lambda i: (i, 0)),
            pl.BlockSpec((bm, D_MODEL), lambda i: (i, 0)),
            pl.BlockSpec((bm, PLE_DIM), lambda i: (i, 0)),
            pl.BlockSpec((D_MODEL, D_MODEL), const2),
            pl.BlockSpec((PLE_DIM, D_MODEL), const2),
            pl.BlockSpec((1, D_MODEL), const2),
        ],
        out_specs=pl.BlockSpec((bm, D_MODEL), lambda i: (i, 0)),
        out_shape=jax.ShapeDtypeStruct((S, D_MODEL), F32),
        compiler_params=_params(("arbitrary",)),
        name="ple_final",
    )(hp, x2, p, wgate, wple, g_final)


def _head_major(w):
    return w.reshape(D_MODEL, 2, N_HEADS, QK_HEAD_DIM).transpose(0, 2, 1, 3).reshape(D_MODEL, 1024)


def kernel(x, p, norm_mix_g, w_in, lambda_q1, lambda_k1, lambda_q2, lambda_k2, subln_g, pool_grp_w, pool_scale, w_attn_br, w_pool_br, w_out, norm_mlp_g, w_mlp_up, w_mlp_down, norm_ple_g, w_ple, w_ple_gate, final_norm_g):
    B, S, D = x.shape
    assert (B, S, D) == (1, SEQ, D_MODEL) and norm_mix_g.shape[0] == 1
    cparts = jnp.asarray(_alibi_parts())

    w = w_in[0]
    wq = _head_major(w[:, 0:1024]).astype(BF16)
    wk = _head_major(w[:, 1024:2048]).astype(BF16)
    wv = w[:, 2048:3072].astype(BF16)
    w_ug = w[:, 3072:].astype(BF16)

    x2d = x[0]
    h, qT, k, vT = _norm_qkv(x2d, norm_mix_g, wq, wk, wv)
    u, gates = _ug(h, w_ug)
    o = _attention(cparts, qT, k, vT, lambda_q1, lambda_k1, lambda_q2, lambda_k2, subln_g)
    x1, hm = _mix(o, u, gates, x2d, pool_grp_w[0].astype(BF16), pool_scale,
                  w_attn_br[0].astype(BF16), w_pool_br[0].astype(BF16), w_out[0].astype(BF16),
                  norm_mlp_g)
    x2, hp = _mlp(hm, x1, w_mlp_up[0].astype(BF16), w_mlp_down[0].astype(BF16), norm_ple_g)
    out = _ple(hp, x2, p[0, 0], w_ple_gate[0].astype(BF16), w_ple[0].astype(BF16),
               final_norm_g.reshape(1, D_MODEL))
    return out[None]
```

```python
import functools
import math

import numpy as np
import jax
import jax.numpy as jnp
from jax import lax
from jax.experimental import pallas as pl
from jax.experimental.pallas import tpu as pltpu

F32 = jnp.float32
BF16 = jnp.bfloat16

D_MODEL = 2048
SEQ = 8192
N_HEADS = 8
V_HEAD_DIM = 128
QK_HEAD_DIM = 64
ATT_WIDTH = N_HEADS * V_HEAD_DIM
POOL_WIDTH = 1024
POOL_WINDOWS = (2, 4, 8, 16)
POOL_GROUP_WIDTH = POOL_WIDTH // len(POOL_WINDOWS)
POOL_HALO = 16
D_FF = 4 * D_MODEL
PLE_DIM = 256
NORM_EPS = 1e-6
LAMBDA_INIT = 0.8 - 0.6 * math.exp(-0.3 * 0)
NEG_BIG = -1e30
LOG2E = math.log2(math.e)
V_ROWS = V_HEAD_DIM + 16
BIAS_TERMS = 3


def _alibi_parts():
    import ml_dtypes
    rest = 2.0 ** (-8.0 * np.arange(1, N_HEADS + 1) / N_HEADS) * LOG2E
    parts = []
    for _ in range(BIAS_TERMS):
        piece = rest.astype(ml_dtypes.bfloat16).astype(np.float64)
        parts.append(piece)
        rest = rest - piece
    return np.stack(parts, axis=1).reshape(-1).astype(np.float32)

ATT_TQ = 256
ATT_TK = 256
ATT_G = 4
VMEM_LIMIT = 56 * 1024 * 1024


def _rms(xf, g):
    return xf * lax.rsqrt(jnp.mean(xf * xf, axis=-1, keepdims=True) + NORM_EPS) * g


def _dot(a, b):
    return jnp.dot(a, b, preferred_element_type=F32)


def _params(sem, flags=None):
    return pltpu.CompilerParams(dimension_semantics=sem, vmem_limit_bytes=VMEM_LIMIT, flags=flags)


def _norm_qkv_kernel(x_ref, g_ref, wq_ref, wk_ref, wv_ref, h_ref, qT_ref, k_ref, vT_ref, *, bm):
    h = _rms(x_ref[...], g_ref[...]).astype(BF16)
    h_ref[...] = h
    q = _dot(h, wq_ref[...]) * (QK_HEAD_DIM ** -0.5 * LOG2E)
    for hh in range(N_HEADS):
        for r in range(bm // ATT_TQ):
            blk = q[r * ATT_TQ:(r + 1) * ATT_TQ, hh * 128:(hh + 1) * 128]
            qT_ref[hh, r] = blk.T.astype(BF16)
    k_ref[...] = _dot(h, wk_ref[...]).astype(BF16)
    v = _dot(h, wv_ref[...])
    ones = jnp.ones((V_ROWS - V_HEAD_DIM, ATT_TK), BF16)
    for hh in range(N_HEADS):
        for r in range(bm // ATT_TK):
            blk = v[r * ATT_TK:(r + 1) * ATT_TK, hh * 128:(hh + 1) * 128]
            vT_ref[hh, r, 0:V_HEAD_DIM, :] = blk.T.astype(BF16)
            vT_ref[hh, r, V_HEAD_DIM:V_ROWS, :] = ones


def _norm_qkv(x, g, wq, wk, wv, bm=512):
    S = x.shape[0]
    nq, nk = S // ATT_TQ, S // ATT_TK
    const = lambda i: (0, 0)
    return pl.pallas_call(
        functools.partial(_norm_qkv_kernel, bm=bm),
        grid=(S // bm,),
        in_specs=[
            pl.BlockSpec((bm, D_MODEL), lambda i: (i, 0)),
            pl.BlockSpec((1, D_MODEL), const),
            pl.BlockSpec((D_MODEL, 1024), const),
            pl.BlockSpec((D_MODEL, 1024), const),
            pl.BlockSpec((D_MODEL, 1024), const),
        ],
        out_specs=[
            pl.BlockSpec((bm, D_MODEL), lambda i: (i, 0)),
            pl.BlockSpec((N_HEADS, bm // ATT_TQ, 128, ATT_TQ), lambda i: (0, i, 0, 0)),
            pl.BlockSpec((bm, 1024), lambda i: (i, 0)),
            pl.BlockSpec((N_HEADS, bm // ATT_TK, V_ROWS, ATT_TK), lambda i: (0, i, 0, 0)),
        ],
        out_shape=[
            jax.ShapeDtypeStruct((S, D_MODEL), BF16),
            jax.ShapeDtypeStruct((N_HEADS, nq, 128, ATT_TQ), BF16),
            jax.ShapeDtypeStruct((S, 1024), BF16),
            jax.ShapeDtypeStruct((N_HEADS, nk, V_ROWS, ATT_TK), BF16),
        ],
        compiler_params=_params(("arbitrary",)),
        name="norm_qkv",
    )(x, g, wq, wk, wv)


def _ug_kernel(h_ref, w_ref, u_ref, gate_ref):
    j = pl.program_id(1)
    z = _dot(h_ref[...], w_ref[...])

    @pl.when(j == 0)
    def _():
        u_ref[...] = z

    @pl.when(j > 0)
    def _():
        gate_ref[...] = jax.nn.sigmoid(z).astype(BF16)


def _ug(h, w_ug, bm=1024, bn=1024):
    S = h.shape[0]
    n = w_ug.shape[1]
    return pl.pallas_call(
        _ug_kernel,
        grid=(S // bm, n // bn),
        in_specs=[
            pl.BlockSpec((bm, D_MODEL), lambda i, j: (i, 0)),
            pl.BlockSpec((D_MODEL, bn), lambda i, j: (0, j)),
        ],
        out_specs=[
            pl.BlockSpec((bm, POOL_WIDTH), lambda i, j: (i, 0)),
            pl.BlockSpec((bm, bn), lambda i, j: (i, jnp.maximum(j - 1, 0))),
        ],
        out_shape=[
            jax.ShapeDtypeStruct((S, POOL_WIDTH), F32),
            jax.ShapeDtypeStruct((S, 2 * D_MODEL), BF16),
        ],
        compiler_params=_params(("arbitrary", "arbitrary")),
        name="norm_ug",
    )(h, w_ug)


def _attn_kernel(cpart_ref, qT_ref, k_ref, vT_ref, lq1_ref, lk1_ref, lq2_ref, lk2_ref, g_ref,
                 o_ref, qs_ref, acc_ref, p_ref, feat_ref, s_ref):
    hg = pl.program_id(0)
    qi = pl.program_id(1)
    tq, tk = ATT_TQ, ATT_TK

    krow = lax.broadcasted_iota(jnp.int32, (tk, 128), 0)
    klane = lax.broadcasted_iota(jnp.int32, (tk, 128), 1)
    feat = jnp.where(klane < BIAS_TERMS, krow >> 4, jnp.where(klane < 2 * BIAS_TERMS, krow & 15, 0))
    feat_ref[...] = feat.astype(F32).astype(BF16)

    row = lax.broadcasted_iota(jnp.int32, (128, tq), 0)
    brow = lax.broadcasted_iota(jnp.int32, (128, 2 * tq), 0)
    slopes = []
    for g in range(ATT_G):
        qT = qT_ref[g]
        zero = jnp.zeros_like(qT)
        qs_ref[g, 0:128, :tq] = jnp.where(row < QK_HEAD_DIM, qT, zero)
        qs_ref[g, 0:128, tq:] = jnp.where(row >= QK_HEAD_DIM, qT, zero)
        parts = [cpart_ref[(hg * ATT_G + g) * BIAS_TERMS + t] for t in range(BIAS_TERMS)]
        bias_rows = jnp.zeros((128, 2 * tq), F32)
        for t in range(BIAS_TERMS):
            bias_rows = jnp.where(brow == t, 16.0 * parts[t], bias_rows)
            bias_rows = jnp.where(brow == BIAS_TERMS + t, parts[t], bias_rows)
        qs_ref[g, 128:256, :] = bias_rows.astype(BF16)
        slopes.append(sum(parts[1:], parts[0]))
    acc_ref[...] = jnp.zeros_like(acc_ref)
    p_ref[...] = jnp.zeros_like(p_ref)

    def load_k(j):
        start = pl.multiple_of(j * tk, tk)
        return [k_ref[pl.ds(start, tk), g * 128:(g + 1) * 128] for g in range(ATT_G)]

    def load_v(j):
        return [vT_ref[g, j] for g in range(ATT_G)]

    def produce(k_blks, par):
        mblks = []
        for g in range(ATT_G):
            k_aug = jnp.concatenate([k_blks[g], feat_ref[...]], axis=1)
            sT = _dot(k_aug, qs_ref[g])
            s_ref[par, g] = sT
            mblks.append(jnp.max(sT, axis=0, keepdims=True))
        return tuple(mblks)

    def pv(v_blks, alphas):
        for g in range(ATT_G):
            acc_ref[g] = alphas[g] * acc_ref[g] + _dot(v_blks[g], p_ref[g])

    def consume(j, sTs, mblks, ms, valid):
        new_ms, new_alphas = [], []
        for g in range(ATT_G):
            off = slopes[g] * (j * tk - qi * tq).astype(F32)
            cand = mblks[g] + off
            if valid is not None:
                cand = jnp.where(valid, cand, NEG_BIG)
            m_new = jnp.maximum(ms[g], cand)
            shift = m_new - off
            if valid is not None:
                shift = jnp.where(valid, shift, -NEG_BIG)
            new_alphas.append(jnp.exp2(ms[g] - m_new))
            new_ms.append(m_new)
            p_ref[g] = jnp.exp2(sTs[g] - shift).astype(BF16)
        return tuple(new_ms), tuple(new_alphas)

    def half_step(j, par, ms, alphas, mblks, valid):
        k_blks = load_k(jnp.minimum(j + 1, vT_ref.shape[1] - 1))
        v_blks = load_v(jnp.maximum(j - 1, 0))
        mblks_next = produce(k_blks, 1 - par)
        pv(v_blks, alphas)
        ms, alphas = consume(j, [s_ref[par, g] for g in range(ATT_G)], mblks, ms, valid)
        return ms, alphas, mblks_next

    def body(t, carry):
        carry = half_step(2 * t, 0, *carry, None)
        return half_step(2 * t + 1, 1, *carry, 2 * t + 1 < qi)

    mblks0 = produce(load_k(0), 0)
    init = (tuple(jnp.full((1, 2 * tq), NEG_BIG, F32) for _ in range(ATT_G)),
            tuple(jnp.ones((1, 2 * tq), F32) for _ in range(ATT_G)),
            mblks0)
    ms, alphas, _ = lax.fori_loop(0, (qi + 1) // 2, body, init)

    pv(load_v(jnp.maximum(qi - 1, 0)), alphas)
    kpos = lax.broadcasted_iota(jnp.int32, (tk, 2 * tq), 0)
    qcol = lax.broadcasted_iota(jnp.int32, (tk, 2 * tq), 1)
    causal = kpos <= jnp.where(qcol >= tq, qcol - tq, qcol)
    par_last = qi % 2
    sTs = [jnp.where(causal, s_ref[par_last, g], NEG_BIG) for g in range(ATT_G)]
    mblks = [jnp.max(sT, axis=0, keepdims=True) for sT in sTs]
    ms, alphas = consume(qi, sTs, mblks, ms, None)
    pv(load_v(qi), alphas)

    lam = (jnp.exp(jnp.sum(lq1_ref[...] * lk1_ref[...], axis=-1, keepdims=True))
           - jnp.exp(jnp.sum(lq2_ref[...] * lk2_ref[...], axis=-1, keepdims=True))
           + LAMBDA_INIT)
    for g in range(ATT_G):
        acc = acc_ref[g]
        inv_l = 1.0 / acc[V_HEAD_DIM:V_HEAD_DIM + 1, :]
        num = acc[:V_HEAD_DIM, :] * inv_l
        oT = num[:, :tq] - lam * num[:, tq:]
        o = _rms(oT.T, g_ref[...]) * (1.0 - LAMBDA_INIT)
        o_ref[:, g * 128:(g + 1) * 128] = o.astype(BF16)


def _attention(cparts, qT, k, vT, lq1, lk1, lq2, lk2, subln_g):
    S = k.shape[0]
    nq, nk = S // ATT_TQ, S // ATT_TK
    G = ATT_G
    vec = lambda h, i: (0, 0)
    return pl.pallas_call(
        _attn_kernel,
        grid=(N_HEADS // G, nq),
        in_specs=[
            pl.BlockSpec(memory_space=pltpu.SMEM),
            pl.BlockSpec((G, None, 128, ATT_TQ), lambda h, i: (h, i, 0, 0)),
            pl.BlockSpec((S, G * 128), lambda h, i: (0, h)),
            pl.BlockSpec((G, nk, V_ROWS, ATT_TK), lambda h, i: (h, 0, 0, 0)),
            pl.BlockSpec((1, QK_HEAD_DIM), vec),
            pl.BlockSpec((1, QK_HEAD_DIM), vec),
            pl.BlockSpec((1, QK_HEAD_DIM), vec),
            pl.BlockSpec((1, QK_HEAD_DIM), vec),
            pl.BlockSpec((1, V_HEAD_DIM), vec),
        ],
        out_specs=pl.BlockSpec((ATT_TQ, G * 128), lambda h, i: (i, h)),
        out_shape=jax.ShapeDtypeStruct((S, ATT_WIDTH), BF16),
        scratch_shapes=[
            pltpu.VMEM((G, 256, 2 * ATT_TQ), BF16),
            pltpu.VMEM((G, V_ROWS, 2 * ATT_TQ), F32),
            pltpu.VMEM((G, ATT_TK, 2 * ATT_TQ), BF16),
            pltpu.VMEM((ATT_TK, 128), BF16),
            pltpu.VMEM((2, G, ATT_TK, 2 * ATT_TQ), F32),
        ],
        compiler_params=_params(("arbitrary", "arbitrary")),
        name="diff_attn",
    )(cparts, qT, k, vT, lq1, lk1, lq2, lk2, subln_g)


def _mix_kernel(o_ref, u_ref, uprev_ref, gate_ref, x_ref, wg_ref, ps_ref, wa_ref, wp_ref, wo_ref,
                gm_ref, x1_ref, hm_ref, ext_ref, *, bm):
    i = pl.program_id(0)
    u = u_ref[...]
    halo = uprev_ref[...]
    ext_ref[0:POOL_HALO, :] = jnp.where(i == 0, jnp.zeros_like(halo), halo)
    ext_ref[POOL_HALO:POOL_HALO + bm, :] = u
    t = i * bm + lax.broadcasted_iota(jnp.int32, (bm, 1), 0)
    pms = []
    for g, w in enumerate(POOL_WINDOWS):
        c0, c1 = g * POOL_GROUP_WIDTH, (g + 1) * POOL_GROUP_WIDTH
        ug = u[:, c0:c1]
        win = ug
        for d in range(1, w):
            win = win + ext_ref[POOL_HALO - d:POOL_HALO - d + bm, c0:c1]
        count = jnp.minimum(t + 1, w).astype(F32)
        pooled = win / count - ug
        pms.append(_dot(pooled.astype(BF16), wg_ref[g]))
    pm = (jnp.concatenate(pms, axis=-1) * ps_ref[...]).astype(BF16)
    p_branch = _dot(pm, wp_ref[...])
    a_branch = _dot(o_ref[...], wa_ref[...])
    gates = gate_ref[...]
    merged = gates[:, :D_MODEL].astype(F32) * a_branch + gates[:, D_MODEL:].astype(F32) * p_branch
    x1 = x_ref[...] + _dot(merged.astype(BF16), wo_ref[...])
    x1_ref[...] = x1
    hm_ref[...] = _rms(x1, gm_ref[...]).astype(BF16)


def _mix(o, u, gates, x, wg, pool_scale, wa, wp, wo, g_mlp, bm=256):
    S = x.shape[0]
    const2 = lambda i: (0, 0)
    once = pl.Buffered(1)
    halo_blocks = bm // POOL_HALO
    return pl.pallas_call(
        functools.partial(_mix_kernel, bm=bm),
        grid=(S // bm,),
        in_specs=[
            pl.BlockSpec((bm, ATT_WIDTH), lambda i: (i, 0)),
            pl.BlockSpec((bm, POOL_WIDTH), lambda i: (i, 0)),
            pl.BlockSpec((POOL_HALO, POOL_WIDTH), lambda i: (jnp.maximum(i * halo_blocks - 1, 0), 0)),
            pl.BlockSpec((bm, 2 * D_MODEL), lambda i: (i, 0)),
            pl.BlockSpec((bm, D_MODEL), lambda i: (i, 0)),
            pl.BlockSpec((len(POOL_WINDOWS), POOL_GROUP_WIDTH, POOL_GROUP_WIDTH), lambda i: (0, 0, 0),
                         pipeline_mode=once),
            pl.BlockSpec((1, POOL_WIDTH), const2),
            pl.BlockSpec((ATT_WIDTH, D_MODEL), const2, pipeline_mode=once),
            pl.BlockSpec((POOL_WIDTH, D_MODEL), const2, pipeline_mode=once),
            pl.BlockSpec((D_MODEL, D_MODEL), const2, pipeline_mode=once),
            pl.BlockSpec((1, D_MODEL), const2),
        ],
        out_specs=[
            pl.BlockSpec((bm, D_MODEL), lambda i: (i, 0)),
            pl.BlockSpec((bm, D_MODEL), lambda i: (i, 0)),
        ],
        out_shape=[
            jax.ShapeDtypeStruct((S, D_MODEL), F32),
            jax.ShapeDtypeStruct((S, D_MODEL), BF16),
        ],
        scratch_shapes=[pltpu.VMEM((POOL_HALO + bm, POOL_WIDTH), F32)],
        compiler_params=_params(("arbitrary",)),
        name="mix_out",
    )(o, u, u, gates, x, wg, pool_scale, wa, wp, wo, g_mlp)


def _mlp_kernel(hm_ref, x1_ref, wu_ref, wd_ref, gp_ref, x2_ref, hp_ref, acc_ref):
    f = pl.program_id(1)

    @pl.when(f == 0)
    def _():
        acc_ref[...] = x1_ref[...]

    a = jnp.maximum(_dot(hm_ref[...], wu_ref[...]), 0.0)
    acc_ref[...] += _dot((a * a).astype(BF16), wd_ref[...])

    @pl.when(f == pl.num_programs(1) - 1)
    def _():
        x2 = acc_ref[...]
        x2_ref[...] = x2
        hp_ref[...] = _rms(x2, gp_ref[...]).astype(BF16)


def _mlp(hm, x1, wu, wd, g_ple, bm=512, bf=1024):
    S = hm.shape[0]
    return pl.pallas_call(
        _mlp_kernel,
        grid=(S // bm, D_FF // bf),
        in_specs=[
            pl.BlockSpec((bm, D_MODEL), lambda i, f: (i, 0)),
            pl.BlockSpec((bm, D_MODEL), lambda i, f: (i, 0)),
            pl.BlockSpec((D_MODEL, bf), lambda i, f: (0, f)),
            pl.BlockSpec((bf, D_MODEL), lambda i, f: (f, 0)),
            pl.BlockSpec((1, D_MODEL), lambda i, f: (0, 0)),
        ],
        out_specs=[
            pl.BlockSpec((bm, D_MODEL), lambda i, f: (i, 0)),
            pl.BlockSpec((bm, D_MODEL), lambda i, f: (i, 0)),
        ],
        out_shape=[
            jax.ShapeDtypeStruct((S, D_MODEL), F32),
            jax.ShapeDtypeStruct((S, D_MODEL), BF16),
        ],
        scratch_shapes=[pltpu.VMEM((bm, D_MODEL), F32)],
        compiler_params=_params(("arbitrary", "arbitrary")),
        name="mlp",
    )(hm, x1, wu, wd, g_ple)


def _ple_kernel(hp_ref, x2_ref, p_ref, wgate_ref, wple_ref, gf_ref, out_ref):
    gate = jax.nn.sigmoid(_dot(hp_ref[...], wgate_ref[...]))
    emb = _dot(p_ref[...].astype(BF16), wple_ref[...])
    x3 = x2_ref[...] + emb * gate
    out_ref[...] = _rms(x3, gf_ref[...])


def _ple(hp, x2, p, wgate, wple, g_final, bm=512):
    S = hp.shape[0]
    const2 = lambda i: (0, 0)
    return pl.pallas_call(
        _ple_kernel,
        grid=(S // bm,),
        in_specs=[
            pl.BlockSpec((bm, D_MODEL), lambda i: (i, 0)),
            pl.BlockSpec((bm, D_MODEL), lambda i: (i, 0)),
            pl.BlockSpec((bm, PLE_DIM), lambda i: (i, 0)),
            pl.BlockSpec((D_MODEL, D_MODEL), const2),
            pl.BlockSpec((PLE_DIM, D_MODEL), const2),
            pl.BlockSpec((1, D_MODEL), const2),
        ],
        out_specs=pl.BlockSpec((bm, D_MODEL), lambda i: (i, 0)),
        out_shape=jax.ShapeDtypeStruct((S, D_MODEL), F32),
        compiler_params=_params(("arbitrary",)),
        name="ple_final",
    )(hp, x2, p, wgate, wple, g_final)


def _head_major(w):
    return w.reshape(D_MODEL, 2, N_HEADS, QK_HEAD_DIM).transpose(0, 2, 1, 3).reshape(D_MODEL, 1024)


def kernel(x, p, norm_mix_g, w_in, lambda_q1, lambda_k1, lambda_q2, lambda_k2, subln_g, pool_grp_w, pool_scale, w_attn_br, w_pool_br, w_out, norm_mlp_g, w_mlp_up, w_mlp_down, norm_ple_g, w_ple, w_ple_gate, final_norm_g):
    B, S, D = x.shape
    assert (B, S, D) == (1, SEQ, D_MODEL) and norm_mix_g.shape[0] == 1
    cparts = jnp.asarray(_alibi_parts())

    w = w_in[0]
    wq = _head_major(w[:, 0:1024]).astype(BF16)
    wk = _head_major(w[:, 1024:2048]).astype(BF16)
    wv = w[:, 2048:3072].astype(BF16)
    w_ug = w[:, 3072:].astype(BF16)

    x2d = x[0]
    h, qT, k, vT = _norm_qkv(x2d, norm_mix_g, wq, wk, wv)
    u, gates = _ug(h, w_ug)
    o = _attention(cparts, qT, k, vT, lambda_q1, lambda_k1, lambda_q2, lambda_k2, subln_g)
    x1, hm = _mix(o, u, gates, x2d, pool_grp_w[0].astype(BF16), pool_scale,
                  w_attn_br[0].astype(BF16), w_pool_br[0].astype(BF16), w_out[0].astype(BF16),
                  norm_mlp_g)
    x2, hp = _mlp(hm, x1, w_mlp_up[0].astype(BF16), w_mlp_down[0].astype(BF16), norm_ple_g)
    out = _ple(hp, x2, p[0, 0], w_ple_gate[0].astype(BF16), w_ple[0].astype(BF16),
               final_norm_g.reshape(1, D_MODEL))
    return out[None]
```

```python
import functools
import math

import numpy as np
import jax
import jax.numpy as jnp
from jax import lax
from jax.experimental import pallas as pl
from jax.experimental.pallas import tpu as pltpu

F32 = jnp.float32
BF16 = jnp.bfloat16

D_MODEL = 2048
SEQ = 8192
N_HEADS = 8
V_HEAD_DIM = 128
QK_HEAD_DIM = 64
ATT_WIDTH = N_HEADS * V_HEAD_DIM
POOL_WIDTH = 1024
POOL_WINDOWS = (2, 4, 8, 16)
POOL_GROUP_WIDTH = POOL_WIDTH // len(POOL_WINDOWS)
POOL_HALO = 16
D_FF = 4 * D_MODEL
PLE_DIM = 256
NORM_EPS = 1e-6
LAMBDA_INIT = 0.8 - 0.6 * math.exp(-0.3 * 0)
NEG_BIG = -1e30
LOG2E = math.log2(math.e)
V_ROWS = V_HEAD_DIM + 16
BIAS_TERMS = 3


def _alibi_parts():
    import ml_dtypes
    rest = 2.0 ** (-8.0 * np.arange(1, N_HEADS + 1) / N_HEADS) * LOG2E
    parts = []
    for _ in range(BIAS_TERMS):
        piece = rest.astype(ml_dtypes.bfloat16).astype(np.float64)
        parts.append(piece)
        rest = rest - piece
    return np.stack(parts, axis=1).reshape(-1).astype(np.float32)

ATT_TQ = 256
ATT_TK = 256
ATT_G = 4
VMEM_LIMIT = 56 * 1024 * 1024


def _rms(xf, g):
    return xf * lax.rsqrt(jnp.mean(xf * xf, axis=-1, keepdims=True) + NORM_EPS) * g


def _dot(a, b):
    return jnp.dot(a, b, preferred_element_type=F32)


def _params(sem, flags=None):
    return pltpu.CompilerParams(dimension_semantics=sem, vmem_limit_bytes=VMEM_LIMIT, flags=flags)


def _prenorm_kernel(x_ref, g_ref, h_ref):
    h_ref[...] = _rms(x_ref[...], g_ref[...]).astype(BF16)


def _prenorm(x, g, bm=512):
    S = x.shape[0]
    return pl.pallas_call(
        _prenorm_kernel,
        grid=(S // bm,),
        in_specs=[pl.BlockSpec((bm, D_MODEL), lambda i: (i, 0)),
                  pl.BlockSpec((1, D_MODEL), lambda i: (0, 0))],
        out_specs=pl.BlockSpec((bm, D_MODEL), lambda i: (i, 0)),
        out_shape=jax.ShapeDtypeStruct((S, D_MODEL), BF16),
        compiler_params=_params(("arbitrary",)),
        name="prenorm",
    )(x, g)


IN_PHASE_COLS = 1024
IN_CHUNK = 256
N_IN_PHASES = 8
GATE_PHASE0 = 4


def _inproj_kernel(h_ref, w_ref, qT_ref, k_ref, vT_ref, u_ref, gate_ref, wb_ref, *, bm):
    j = pl.program_id(0)
    i = pl.program_id(1)
    half = IN_PHASE_COLS // 2

    @pl.when(jnp.logical_and(i == 0, j != 1))
    def _():
        wb_ref[...] = w_ref[...].astype(BF16)

    @pl.when(jnp.logical_and(i == 0, j == 1))
    def _():
        for hh in range(N_HEADS):
            for m in range(2):
                src = m * half + hh * QK_HEAD_DIM
                dst = hh * 128 + m * QK_HEAD_DIM
                wb_ref[:, dst:dst + QK_HEAD_DIM] = w_ref[:, src:src + QK_HEAD_DIM].astype(BF16)

    def chunks():
        h = h_ref[...]
        for c in range(IN_PHASE_COLS // IN_CHUNK):
            yield c, _dot(h, wb_ref[:, c * IN_CHUNK:(c + 1) * IN_CHUNK])

    @pl.when(j == 0)
    def _():
        heads_per_chunk = IN_CHUNK // QK_HEAD_DIM
        for c, z in chunks():
            zT = (z * (QK_HEAD_DIM ** -0.5 * LOG2E)).T.astype(BF16)
            m = (c * IN_CHUNK) // half
            h0 = ((c * IN_CHUNK) % half) // QK_HEAD_DIM
            for hl in range(heads_per_chunk):
                for r in range(bm // ATT_TQ):
                    qT_ref[h0 + hl, r, m * QK_HEAD_DIM:(m + 1) * QK_HEAD_DIM, :] = (
                        zT[hl * QK_HEAD_DIM:(hl + 1) * QK_HEAD_DIM, r * ATT_TQ:(r + 1) * ATT_TQ])

    @pl.when(j == 1)
    def _():
        for c, z in chunks():
            k_ref[:, c * IN_CHUNK:(c + 1) * IN_CHUNK] = z.astype(BF16)

    @pl.when(j == 2)
    def _():
        heads_per_chunk = IN_CHUNK // V_HEAD_DIM
        ones = jnp.ones((V_ROWS - V_HEAD_DIM, ATT_TK), BF16)
        for c, z in chunks():
            zT = z.T.astype(BF16)
            for hl in range(heads_per_chunk):
                hh = c * heads_per_chunk + hl
                for r in range(bm // ATT_TK):
                    vT_ref[hh, r, 0:V_HEAD_DIM, :] = (
                        zT[hl * V_HEAD_DIM:(hl + 1) * V_HEAD_DIM, r * ATT_TK:(r + 1) * ATT_TK])
                    vT_ref[hh, r, V_HEAD_DIM:V_ROWS, :] = ones

    @pl.when(j == 3)
    def _():
        for c, z in chunks():
            u_ref[:, c * IN_CHUNK:(c + 1) * IN_CHUNK] = z

    @pl.when(j >= GATE_PHASE0)
    def _():
        for c, z in chunks():
            gate_ref[:, c * IN_CHUNK:(c + 1) * IN_CHUNK] = jax.nn.sigmoid(z).astype(BF16)


def _inproj(h, w_in, bm=512):
    S = h.shape[0]
    n_rows = S // bm
    nq, nk = S // ATT_TQ, S // ATT_TK

    def rows_in_phase(phase):
        def f(j, i):
            return jnp.where(j == phase, i, jnp.where(j < phase, 0, n_rows - 1))
        return f

    rq, rk, rv, ru = (rows_in_phase(ph) for ph in range(4))
    return pl.pallas_call(
        functools.partial(_inproj_kernel, bm=bm),
        grid=(N_IN_PHASES, n_rows),
        in_specs=[
            pl.BlockSpec((bm, D_MODEL), lambda j, i: (i, 0)),
            pl.BlockSpec((D_MODEL, IN_PHASE_COLS), lambda j, i: (0, j)),
        ],
        out_specs=[
            pl.BlockSpec((N_HEADS, bm // ATT_TQ, 128, ATT_TQ), lambda j, i: (0, rq(j, i), 0, 0)),
            pl.BlockSpec((bm, 1024), lambda j, i: (rk(j, i), 0)),
            pl.BlockSpec((N_HEADS, bm // ATT_TK, V_ROWS, ATT_TK), lambda j, i: (0, rv(j, i), 0, 0)),
            pl.BlockSpec((bm, POOL_WIDTH), lambda j, i: (ru(j, i), 0)),
            pl.BlockSpec((bm, IN_PHASE_COLS),
                         lambda j, i: (jnp.where(j >= GATE_PHASE0, i, 0), jnp.maximum(j - GATE_PHASE0, 0))),
        ],
        out_shape=[
            jax.ShapeDtypeStruct((N_HEADS, nq, 128, ATT_TQ), BF16),
            jax.ShapeDtypeStruct((S, 1024), BF16),
            jax.ShapeDtypeStruct((N_HEADS, nk, V_ROWS, ATT_TK), BF16),
            jax.ShapeDtypeStruct((S, POOL_WIDTH), F32),
            jax.ShapeDtypeStruct((S, 2 * D_MODEL), BF16),
        ],
        scratch_shapes=[pltpu.VMEM((D_MODEL, IN_PHASE_COLS), BF16)],
        compiler_params=_params(("arbitrary", "arbitrary")),
        name="inproj",
    )(h, w_in)


def _attn_kernel(cpart_ref, qT_ref, k_ref, vT_ref, lq1_ref, lk1_ref, lq2_ref, lk2_ref, g_ref,
                 o_ref, qs_ref, acc_ref, p_ref, feat_ref, s_ref):
    hg = pl.program_id(0)
    qi = pl.program_id(1)
    tq, tk = ATT_TQ, ATT_TK

    krow = lax.broadcasted_iota(jnp.int32, (tk, 128), 0)
    klane = lax.broadcasted_iota(jnp.int32, (tk, 128), 1)
    feat = jnp.where(klane < BIAS_TERMS, krow >> 4, jnp.where(klane < 2 * BIAS_TERMS, krow & 15, 0))
    feat_ref[...] = feat.astype(F32).astype(BF16)

    row = lax.broadcasted_iota(jnp.int32, (128, tq), 0)
    brow = lax.broadcasted_iota(jnp.int32, (128, 2 * tq), 0)
    slopes = []
    for g in range(ATT_G):
        qT = qT_ref[g]
        zero = jnp.zeros_like(qT)
        qs_ref[g, 0:128, :tq] = jnp.where(row < QK_HEAD_DIM, qT, zero)
        qs_ref[g, 0:128, tq:] = jnp.where(row >= QK_HEAD_DIM, qT, zero)
        parts = [cpart_ref[(hg * ATT_G + g) * BIAS_TERMS + t] for t in range(BIAS_TERMS)]
        bias_rows = jnp.zeros((128, 2 * tq), F32)
        for t in range(BIAS_TERMS):
            bias_rows = jnp.where(brow == t, 16.0 * parts[t], bias_rows)
            bias_rows = jnp.where(brow == BIAS_TERMS + t, parts[t], bias_rows)
        qs_ref[g, 128:256, :] = bias_rows.astype(BF16)
        slopes.append(sum(parts[1:], parts[0]))
    acc_ref[...] = jnp.zeros_like(acc_ref)
    p_ref[...] = jnp.zeros_like(p_ref)

    def load_k(j):
        start = pl.multiple_of(j * tk, tk)
        return [k_ref[pl.ds(start, tk), g * 128:(g + 1) * 128] for g in range(ATT_G)]

    def load_v(j):
        return [vT_ref[g, j] for g in range(ATT_G)]

    def produce(k_blks, par):
        mblks = []
        for g in range(ATT_G):
            k_aug = jnp.concatenate([k_blks[g], feat_ref[...]], axis=1)
            sT = _dot(k_aug, qs_ref[g])
            s_ref[par, g] = sT
            mblks.append(jnp.max(sT, axis=0, keepdims=True))
        return tuple(mblks)

    def pv(v_blks, alphas):
        for g in range(ATT_G):
            acc_ref[g] = alphas[g] * acc_ref[g] + _dot(v_blks[g], p_ref[g])

    def consume(j, sTs, mblks, ms, valid):
        new_ms, new_alphas = [], []
        for g in range(ATT_G):
            off = slopes[g] * (j * tk - qi * tq).astype(F32)
            cand = mblks[g] + off
            if valid is not None:
                cand = jnp.where(valid, cand, NEG_BIG)
            m_new = jnp.maximum(ms[g], cand)
            shift = m_new - off
            if valid is not None:
                shift = jnp.where(valid, shift, -NEG_BIG)
            new_alphas.append(jnp.exp2(ms[g] - m_new))
            new_ms.append(m_new)
            p_ref[g] = jnp.exp2(sTs[g] - shift).astype(BF16)
        return tuple(new_ms), tuple(new_alphas)

    def half_step(j, par, ms, alphas, mblks, valid):
        k_blks = load_k(jnp.minimum(j + 1, vT_ref.shape[1] - 1))
        v_blks = load_v(jnp.maximum(j - 1, 0))
        mblks_next = produce(k_blks, 1 - par)
        pv(v_blks, alphas)
        ms, alphas = consume(j, [s_ref[par, g] for g in range(ATT_G)], mblks, ms, valid)
        return ms, alphas, mblks_next

    def body(t, carry):
        carry = half_step(2 * t, 0, *carry, None)
        return half_step(2 * t + 1, 1, *carry, 2 * t + 1 < qi)

    mblks0 = produce(load_k(0), 0)
    init = (tuple(jnp.full((1, 2 * tq), NEG_BIG, F32) for _ in range(ATT_G)),
            tuple(jnp.ones((1, 2 * tq), F32) for _ in range(ATT_G)),
            mblks0)
    ms, alphas, _ = lax.fori_loop(0, (qi + 1) // 2, body, init)

    pv(load_v(jnp.maximum(qi - 1, 0)), alphas)
    kpos = lax.broadcasted_iota(jnp.int32, (tk, 2 * tq), 0)
    qcol = lax.broadcasted_iota(jnp.int32, (tk, 2 * tq), 1)
    causal = kpos <= jnp.where(qcol >= tq, qcol - tq, qcol)
    par_last = qi % 2
    sTs = [jnp.where(causal, s_ref[par_last, g], NEG_BIG) for g in range(ATT_G)]
    mblks = [jnp.max(sT, axis=0, keepdims=True) for sT in sTs]
    ms, alphas = consume(qi, sTs, mblks, ms, None)
    pv(load_v(qi), alphas)

    lam = (jnp.exp(jnp.sum(lq1_ref[...] * lk1_ref[...], axis=-1, keepdims=True))
           - jnp.exp(jnp.sum(lq2_ref[...] * lk2_ref[...], axis=-1, keepdims=True))
           + LAMBDA_INIT)
    for g in range(ATT_G):
        acc = acc_ref[g]
        inv_l = 1.0 / acc[V_HEAD_DIM:V_HEAD_DIM + 1, :]
        num = acc[:V_HEAD_DIM, :] * inv_l
        oT = num[:, :tq] - lam * num[:, tq:]
        o = _rms(oT.T, g_ref[...]) * (1.0 - LAMBDA_INIT)
        o_ref[:, g * 128:(g + 1) * 128] = o.astype(BF16)


def _attention(cparts, qT, k, vT, lq1, lk1, lq2, lk2, subln_g):
    S = k.shape[0]
    nq, nk = S // ATT_TQ, S // ATT_TK
    G = ATT_G
    vec = lambda h, i: (0, 0)
    return pl.pallas_call(
        _attn_kernel,
        grid=(N_HEADS // G, nq),
        in_specs=[
            pl.BlockSpec(memory_space=pltpu.SMEM),
            pl.BlockSpec((G, None, 128, ATT_TQ), lambda h, i: (h, i, 0, 0)),
            pl.BlockSpec((S, G * 128), lambda h, i: (0, h)),
            pl.BlockSpec((G, nk, V_ROWS, ATT_TK), lambda h, i: (h, 0, 0, 0)),
            pl.BlockSpec((1, QK_HEAD_DIM), vec),
            pl.BlockSpec((1, QK_HEAD_DIM), vec),
            pl.BlockSpec((1, QK_HEAD_DIM), vec),
            pl.BlockSpec((1, QK_HEAD_DIM), vec),
            pl.BlockSpec((1, V_HEAD_DIM), vec),
        ],
        out_specs=pl.BlockSpec((ATT_TQ, G * 128), lambda h, i: (i, h)),
        out_shape=jax.ShapeDtypeStruct((S, ATT_WIDTH), BF16),
        scratch_shapes=[
            pltpu.VMEM((G, 256, 2 * ATT_TQ), BF16),
            pltpu.VMEM((G, V_ROWS, 2 * ATT_TQ), F32),
            pltpu.VMEM((G, ATT_TK, 2 * ATT_TQ), BF16),
            pltpu.VMEM((ATT_TK, 128), BF16),
            pltpu.VMEM((2, G, ATT_TK, 2 * ATT_TQ), F32),
        ],
        compiler_params=_params(("arbitrary", "arbitrary")),
        name="diff_attn",
    )(cparts, qT, k, vT, lq1, lk1, lq2, lk2, subln_g)


def _mix_kernel(o_ref, u_ref, uprev_ref, gate_ref, x_ref, wg_ref, ps_ref, wa_ref, wp_ref, wo_ref,
                gm_ref, x1_ref, hm_ref, ext_ref, *, bm):
    i = pl.program_id(0)
    u = u_ref[...]
    halo = uprev_ref[...]
    ext_ref[0:POOL_HALO, :] = jnp.where(i == 0, jnp.zeros_like(halo), halo)
    ext_ref[POOL_HALO:POOL_HALO + bm, :] = u
    t = i * bm + lax.broadcasted_iota(jnp.int32, (bm, 1), 0)
    pms = []
    for g, w in enumerate(POOL_WINDOWS):
        c0, c1 = g * POOL_GROUP_WIDTH, (g + 1) * POOL_GROUP_WIDTH
        ug = u[:, c0:c1]
        win = ug
        for d in range(1, w):
            win = win + ext_ref[POOL_HALO - d:POOL_HALO - d + bm, c0:c1]
        count = jnp.minimum(t + 1, w).astype(F32)
        pooled = win / count - ug
        pms.append(_dot(pooled.astype(BF16), wg_ref[g]))
    pm = (jnp.concatenate(pms, axis=-1) * ps_ref[...]).astype(BF16)
    p_branch = _dot(pm, wp_ref[...])
    a_branch = _dot(o_ref[...], wa_ref[...])
    gates = gate_ref[...]
    merged = gates[:, :D_MODEL].astype(F32) * a_branch + gates[:, D_MODEL:].astype(F32) * p_branch
    x1 = x_ref[...] + _dot(merged.astype(BF16), wo_ref[...])
    x1_ref[...] = x1
    hm_ref[...] = _rms(x1, gm_ref[...]).astype(BF16)


def _mix(o, u, gates, x, wg, pool_scale, wa, wp, wo, g_mlp, bm=256):
    S = x.shape[0]
    const2 = lambda i: (0, 0)
    once = pl.Buffered(1)
    halo_blocks = bm // POOL_HALO
    return pl.pallas_call(
        functools.partial(_mix_kernel, bm=bm),
        grid=(S // bm,),
        in_specs=[
            pl.BlockSpec((bm, ATT_WIDTH), lambda i: (i, 0)),
            pl.BlockSpec((bm, POOL_WIDTH), lambda i: (i, 0)),
            pl.BlockSpec((POOL_HALO, POOL_WIDTH), lambda i: (jnp.maximum(i * halo_blocks - 1, 0), 0)),
            pl.BlockSpec((bm, 2 * D_MODEL), lambda i: (i, 0)),
            pl.BlockSpec((bm, D_MODEL), lambda i: (i, 0)),
            pl.BlockSpec((len(POOL_WINDOWS), POOL_GROUP_WIDTH, POOL_GROUP_WIDTH), lambda i: (0, 0, 0),
                         pipeline_mode=once),
            pl.BlockSpec((1, POOL_WIDTH), const2),
            pl.BlockSpec((ATT_WIDTH, D_MODEL), const2, pipeline_mode=once),
            pl.BlockSpec((POOL_WIDTH, D_MODEL), const2, pipeline_mode=once),
            pl.BlockSpec((D_MODEL, D_MODEL), const2, pipeline_mode=once),
            pl.BlockSpec((1, D_MODEL), const2),
        ],
        out_specs=[
            pl.BlockSpec((bm, D_MODEL), lambda i: (i, 0)),
            pl.BlockSpec((bm, D_MODEL), lambda i: (i, 0)),
        ],
        out_shape=[
            jax.ShapeDtypeStruct((S, D_MODEL), F32),
            jax.ShapeDtypeStruct((S, D_MODEL), BF16),
        ],
        scratch_shapes=[pltpu.VMEM((POOL_HALO + bm, POOL_WIDTH), F32)],
        compiler_params=_params(("arbitrary",)),
        name="mix_out",
    )(o, u, u, gates, x, wg, pool_scale, wa, wp, wo, g_mlp)


def _mlp_kernel(hm_ref, x1_ref, wu_ref, wd_ref, gp_ref, x2_ref, hp_ref, acc_ref):
    f = pl.program_id(1)

    @pl.when(f == 0)
    def _():
        acc_ref[...] = x1_ref[...]

    a = jnp.maximum(_dot(hm_ref[...], wu_ref[...]), 0.0)
    acc_ref[...] += _dot((a * a).astype(BF16), wd_ref[...])

    @pl.when(f == pl.num_programs(1) - 1)
    def _():
        x2 = acc_ref[...]
        x2_ref[...] = x2
        hp_ref[...] = _rms(x2, gp_ref[...]).astype(BF16)


def _mlp(hm, x1, wu, wd, g_ple, bm=512, bf=1024):
    S = hm.shape[0]
    return pl.pallas_call(
        _mlp_kernel,
        grid=(S // bm, D_FF // bf),
        in_specs=[
            pl.BlockSpec((bm, D_MODEL), lambda i, f: (i, 0)),
            pl.BlockSpec((bm, D_MODEL), lambda i, f: (i, 0)),
            pl.BlockSpec((D_MODEL, bf), lambda i, f: (0, f)),
            pl.BlockSpec((bf, D_MODEL), lambda i, f: (f, 0)),
            pl.BlockSpec((1, D_MODEL), lambda i, f: (0, 0)),
        ],
        out_specs=[
            pl.BlockSpec((bm, D_MODEL), lambda i, f: (i, 0)),
            pl.BlockSpec((bm, D_MODEL), lambda i, f: (i, 0)),
        ],
        out_shape=[
            jax.ShapeDtypeStruct((S, D_MODEL), F32),
            jax.ShapeDtypeStruct((S, D_MODEL), BF16),
        ],
        scratch_shapes=[pltpu.VMEM((bm, D_MODEL), F32)],
        compiler_params=_params(("arbitrary", "arbitrary")),
        name="mlp",
    )(hm, x1, wu, wd, g_ple)


def _ple_kernel(hp_ref, x2_ref, p_ref, wgate_ref, wple_ref, gf_ref, out_ref):
    gate = jax.nn.sigmoid(_dot(hp_ref[...], wgate_ref[...]))
    emb = _dot(p_ref[...].astype(BF16), wple_ref[...])
    x3 = x2_ref[...] + emb * gate
    out_ref[...] = _rms(x3, gf_ref[...])


def _ple(hp, x2, p, wgate, wple, g_final, bm=512):
    S = hp.shape[0]
    const2 = lambda i: (0, 0)
    return pl.pallas_call(
        _ple_kernel,
        grid=(S // bm,),
        in_specs=[
            pl.BlockSpec((bm, D_MODEL), lambda i: (i, 0)),
            pl.BlockSpec((bm, D_MODEL), lambda i: (i, 0)),
            pl.BlockSpec((bm, PLE_DIM), lambda i: (i, 0)),
            pl.BlockSpec((D_MODEL, D_MODEL), const2),
            pl.BlockSpec((PLE_DIM, D_MODEL), const2),
            pl.BlockSpec((1, D_MODEL), const2),
        ],
        out_specs=pl.BlockSpec((bm, D_MODEL), lambda i: (i, 0)),
        out_shape=jax.ShapeDtypeStruct((S, D_MODEL), F32),
        compiler_params=_params(("arbitrary",)),
        name="ple_final",
    )(hp, x2, p, wgate, wple, g_final)


def kernel(x, p, norm_mix_g, w_in, lambda_q1, lambda_k1, lambda_q2, lambda_k2, subln_g, pool_grp_w, pool_scale, w_attn_br, w_pool_br, w_out, norm_mlp_g, w_mlp_up, w_mlp_down, norm_ple_g, w_ple, w_ple_gate, final_norm_g):
    B, S, D = x.shape
    assert (B, S, D) == (1, SEQ, D_MODEL) and norm_mix_g.shape[0] == 1
    cparts = jnp.asarray(_alibi_parts())

    x2d = x[0]
    h = _prenorm(x2d, norm_mix_g)
    qT, k, vT, u, gates = _inproj(h, w_in[0])
    o = _attention(cparts, qT, k, vT, lambda_q1, lambda_k1, lambda_q2, lambda_k2, subln_g)
    x1, hm = _mix(o, u, gates, x2d, pool_grp_w[0].astype(BF16), pool_scale,
                  w_attn_br[0].astype(BF16), w_pool_br[0].astype(BF16), w_out[0].astype(BF16),
                  norm_mlp_g)
    x2, hp = _mlp(hm, x1, w_mlp_up[0].astype(BF16), w_mlp_down[0].astype(BF16), norm_ple_g)
    out = _ple(hp, x2, p[0, 0], w_ple_gate[0].astype(BF16), w_ple[0].astype(BF16),
               final_norm_g.reshape(1, D_MODEL))
    return out[None]
```

```python
import functools
import math

import numpy as np
import jax
import jax.numpy as jnp
from jax import lax
from jax.experimental import pallas as pl
from jax.experimental.pallas import tpu as pltpu

F32 = jnp.float32
BF16 = jnp.bfloat16

D_MODEL = 2048
SEQ = 8192
N_HEADS = 8
V_HEAD_DIM = 128
QK_HEAD_DIM = 64
ATT_WIDTH = N_HEADS * V_HEAD_DIM
POOL_WIDTH = 1024
POOL_WINDOWS = (2, 4, 8, 16)
POOL_GROUP_WIDTH = POOL_WIDTH // len(POOL_WINDOWS)
POOL_HALO = 16
D_FF = 4 * D_MODEL
PLE_DIM = 256
NORM_EPS = 1e-6
LAMBDA_INIT = 0.8 - 0.6 * math.exp(-0.3 * 0)
NEG_BIG = -1e30
LOG2E = math.log2(math.e)
V_ROWS = V_HEAD_DIM + 16
BIAS_TERMS = 3


def _alibi_parts():
    import ml_dtypes
    rest = 2.0 ** (-8.0 * np.arange(1, N_HEADS + 1) / N_HEADS) * LOG2E
    parts = []
    for _ in range(BIAS_TERMS):
        piece = rest.astype(ml_dtypes.bfloat16).astype(np.float64)
        parts.append(piece)
        rest = rest - piece
    return np.stack(parts, axis=1).reshape(-1).astype(np.float32)

ATT_TQ = 256
ATT_TK = 256
ATT_G = 4
ATT_PADW = 2 * ATT_TQ + 128
VMEM_LIMIT = 56 * 1024 * 1024


def _rms(xf, g):
    return xf * lax.rsqrt(jnp.mean(xf * xf, axis=-1, keepdims=True) + NORM_EPS) * g


def _dot(a, b):
    return jnp.dot(a, b, preferred_element_type=F32)


def _params(sem, flags=None):
    return pltpu.CompilerParams(dimension_semantics=sem, vmem_limit_bytes=VMEM_LIMIT, flags=flags)


def _prenorm_kernel(x_ref, g_ref, h_ref):
    h_ref[...] = _rms(x_ref[...], g_ref[...]).astype(BF16)


def _prenorm(x, g, bm=512):
    S = x.shape[0]
    return pl.pallas_call(
        _prenorm_kernel,
        grid=(S // bm,),
        in_specs=[pl.BlockSpec((bm, D_MODEL), lambda i: (i, 0)),
                  pl.BlockSpec((1, D_MODEL), lambda i: (0, 0))],
        out_specs=pl.BlockSpec((bm, D_MODEL), lambda i: (i, 0)),
        out_shape=jax.ShapeDtypeStruct((S, D_MODEL), BF16),
        compiler_params=_params(("arbitrary",)),
        name="prenorm",
    )(x, g)


IN_PHASE_COLS = 1024
IN_CHUNK = 256
N_IN_PHASES = 8
GATE_PHASE0 = 4


def _inproj_kernel(h_ref, w_ref, qT_ref, k_ref, vT_ref, u_ref, gate_ref, wb_ref, *, bm):
    j = pl.program_id(0)
    i = pl.program_id(1)
    half = IN_PHASE_COLS // 2

    @pl.when(jnp.logical_and(i == 0, j != 1))
    def _():
        wb_ref[...] = w_ref[...].astype(BF16)

    @pl.when(jnp.logical_and(i == 0, j == 1))
    def _():
        for hh in range(N_HEADS):
            for m in range(2):
                src = m * half + hh * QK_HEAD_DIM
                dst = hh * 128 + m * QK_HEAD_DIM
                wb_ref[:, dst:dst + QK_HEAD_DIM] = w_ref[:, src:src + QK_HEAD_DIM].astype(BF16)

    def chunks():
        h = h_ref[...]
        for c in range(IN_PHASE_COLS // IN_CHUNK):
            yield c, _dot(h, wb_ref[:, c * IN_CHUNK:(c + 1) * IN_CHUNK])

    @pl.when(j == 0)
    def _():
        heads_per_chunk = IN_CHUNK // QK_HEAD_DIM
        for c, z in chunks():
            zT = (z * (QK_HEAD_DIM ** -0.5 * LOG2E)).T.astype(BF16)
            m = (c * IN_CHUNK) // half
            h0 = ((c * IN_CHUNK) % half) // QK_HEAD_DIM
            for hl in range(heads_per_chunk):
                for r in range(bm // ATT_TQ):
                    qT_ref[h0 + hl, r, m * QK_HEAD_DIM:(m + 1) * QK_HEAD_DIM, :] = (
                        zT[hl * QK_HEAD_DIM:(hl + 1) * QK_HEAD_DIM, r * ATT_TQ:(r + 1) * ATT_TQ])

    @pl.when(j == 1)
    def _():
        for c, z in chunks():
            k_ref[:, c * IN_CHUNK:(c + 1) * IN_CHUNK] = z.astype(BF16)

    @pl.when(j == 2)
    def _():
        heads_per_chunk = IN_CHUNK // V_HEAD_DIM
        ones = jnp.ones((V_ROWS - V_HEAD_DIM, ATT_TK), BF16)
        for c, z in chunks():
            zT = z.T.astype(BF16)
            for hl in range(heads_per_chunk):
                hh = c * heads_per_chunk + hl
                for r in range(bm // ATT_TK):
                    vT_ref[hh, r, 0:V_HEAD_DIM, :] = (
                        zT[hl * V_HEAD_DIM:(hl + 1) * V_HEAD_DIM, r * ATT_TK:(r + 1) * ATT_TK])
                    vT_ref[hh, r, V_HEAD_DIM:V_ROWS, :] = ones

    @pl.when(j == 3)
    def _():
        for c, z in chunks():
            u_ref[:, c * IN_CHUNK:(c + 1) * IN_CHUNK] = z

    @pl.when(j >= GATE_PHASE0)
    def _():
        for c, z in chunks():
            gate_ref[:, c * IN_CHUNK:(c + 1) * IN_CHUNK] = jax.nn.sigmoid(z).astype(BF16)


def _inproj(h, w_in, bm=512):
    S = h.shape[0]
    n_rows = S // bm
    nq, nk = S // ATT_TQ, S // ATT_TK

    def rows_in_phase(phase):
        def f(j, i):
            return jnp.where(j == phase, i, jnp.where(j < phase, 0, n_rows - 1))
        return f

    rq, rk, rv, ru = (rows_in_phase(ph) for ph in range(4))
    return pl.pallas_call(
        functools.partial(_inproj_kernel, bm=bm),
        grid=(N_IN_PHASES, n_rows),
        in_specs=[
            pl.BlockSpec((bm, D_MODEL), lambda j, i: (i, 0)),
            pl.BlockSpec((D_MODEL, IN_PHASE_COLS), lambda j, i: (0, j)),
        ],
        out_specs=[
            pl.BlockSpec((N_HEADS, bm // ATT_TQ, 128, ATT_TQ), lambda j, i: (0, rq(j, i), 0, 0)),
            pl.BlockSpec((bm, 1024), lambda j, i: (rk(j, i), 0)),
            pl.BlockSpec((N_HEADS, bm // ATT_TK, V_ROWS, ATT_TK), lambda j, i: (0, rv(j, i), 0, 0)),
            pl.BlockSpec((bm, POOL_WIDTH), lambda j, i: (ru(j, i), 0)),
            pl.BlockSpec((bm, IN_PHASE_COLS),
                         lambda j, i: (jnp.where(j >= GATE_PHASE0, i, 0), jnp.maximum(j - GATE_PHASE0, 0))),
        ],
        out_shape=[
            jax.ShapeDtypeStruct((N_HEADS, nq, 128, ATT_TQ), BF16),
            jax.ShapeDtypeStruct((S, 1024), BF16),
            jax.ShapeDtypeStruct((N_HEADS, nk, V_ROWS, ATT_TK), BF16),
            jax.ShapeDtypeStruct((S, POOL_WIDTH), F32),
            jax.ShapeDtypeStruct((S, 2 * D_MODEL), BF16),
        ],
        scratch_shapes=[pltpu.VMEM((D_MODEL, IN_PHASE_COLS), BF16)],
        compiler_params=_params(("arbitrary", "arbitrary")),
        name="inproj",
    )(h, w_in)


def _attn_kernel(cpart_ref, qT_ref, k_ref, vT_ref, lq1_ref, lk1_ref, lq2_ref, lk2_ref, g_ref,
                 o_ref, qs_ref, acc_ref, p_ref, feat_ref, s_ref):
    hg = pl.program_id(0)
    qi = pl.program_id(1)
    tq, tk = ATT_TQ, ATT_TK

    krow = lax.broadcasted_iota(jnp.int32, (tk, 128), 0)
    klane = lax.broadcasted_iota(jnp.int32, (tk, 128), 1)
    feat = jnp.where(klane < BIAS_TERMS, krow >> 4, jnp.where(klane < 2 * BIAS_TERMS, krow & 15, 0))
    feat_ref[...] = feat.astype(F32).astype(BF16)

    row = lax.broadcasted_iota(jnp.int32, (128, tq), 0)
    brow = lax.broadcasted_iota(jnp.int32, (128, 2 * tq), 0)
    slopes = []
    for g in range(ATT_G):
        qT = qT_ref[g]
        zero = jnp.zeros_like(qT)
        qs_ref[g, 0:128, :tq] = jnp.where(row < QK_HEAD_DIM, qT, zero)
        qs_ref[g, 0:128, tq:] = jnp.where(row >= QK_HEAD_DIM, qT, zero)
        parts = [cpart_ref[(hg * ATT_G + g) * BIAS_TERMS + t] for t in range(BIAS_TERMS)]
        bias_rows = jnp.zeros((128, 2 * tq), F32)
        for t in range(BIAS_TERMS):
            bias_rows = jnp.where(brow == t, 16.0 * parts[t], bias_rows)
            bias_rows = jnp.where(brow == BIAS_TERMS + t, parts[t], bias_rows)
        qs_ref[g, 128:256, :] = bias_rows.astype(BF16)
        slopes.append(sum(parts[1:], parts[0]))
    acc_ref[...] = jnp.zeros_like(acc_ref)
    p_ref[...] = jnp.zeros_like(p_ref)

    def load_k(j):
        start = pl.multiple_of(j * tk, tk)
        return [k_ref[pl.ds(start, tk), g * 128:(g + 1) * 128] for g in range(ATT_G)]

    def load_v(j):
        return [vT_ref[g, j] for g in range(ATT_G)]

    def produce(k_blks, par):
        mblks = []
        for g in range(ATT_G):
            k_aug = jnp.concatenate([k_blks[g], feat_ref[...]], axis=1)
            sT = _dot(k_aug, qs_ref[g])
            s_ref[par, g, :, 0:2 * tq] = sT
            mblks.append(jnp.max(sT, axis=0, keepdims=True))
        return tuple(mblks)

    def pv(v_blks, alphas):
        for g in range(ATT_G):
            acc_ref[g, :, 0:2 * tq] = (alphas[g] * acc_ref[g, :, 0:2 * tq]
                                       + _dot(v_blks[g], p_ref[g, :, 0:2 * tq]))

    def consume(j, sTs, mblks, ms, valid):
        new_ms, new_alphas = [], []
        for g in range(ATT_G):
            off = slopes[g] * (j * tk - qi * tq).astype(F32)
            cand = mblks[g] + off
            if valid is not None:
                cand = jnp.where(valid, cand, NEG_BIG)
            m_new = jnp.maximum(ms[g], cand)
            shift = m_new - off
            if valid is not None:
                shift = jnp.where(valid, shift, -NEG_BIG)
            new_alphas.append(jnp.exp2(ms[g] - m_new))
            new_ms.append(m_new)
            p_ref[g, :, 0:2 * tq] = jnp.exp2(sTs[g] - shift).astype(BF16)
        return tuple(new_ms), tuple(new_alphas)

    def half_step(j, par, ms, alphas, mblks, valid):
        k_blks = load_k(jnp.minimum(j + 1, vT_ref.shape[1] - 1))
        v_blks = load_v(jnp.maximum(j - 1, 0))
        mblks_next = produce(k_blks, 1 - par)
        pv(v_blks, alphas)
        ms, alphas = consume(j, [s_ref[par, g, :, 0:2 * tq] for g in range(ATT_G)], mblks, ms, valid)
        return ms, alphas, mblks_next

    def body(t, carry):
        carry = half_step(2 * t, 0, *carry, None)
        return half_step(2 * t + 1, 1, *carry, 2 * t + 1 < qi)

    mblks0 = produce(load_k(0), 0)
    init = (tuple(jnp.full((1, 2 * tq), NEG_BIG, F32) for _ in range(ATT_G)),
            tuple(jnp.ones((1, 2 * tq), F32) for _ in range(ATT_G)),
            mblks0)
    ms, alphas, _ = lax.fori_loop(0, (qi + 1) // 2, body, init)

    pv(load_v(jnp.maximum(qi - 1, 0)), alphas)
    kpos = lax.broadcasted_iota(jnp.int32, (tk, 2 * tq), 0)
    qcol = lax.broadcasted_iota(jnp.int32, (tk, 2 * tq), 1)
    causal = kpos <= jnp.where(qcol >= tq, qcol - tq, qcol)
    par_last = qi % 2
    sTs = [jnp.where(causal, s_ref[par_last, g, :, 0:2 * tq], NEG_BIG) for g in range(ATT_G)]
    mblks = [jnp.max(sT, axis=0, keepdims=True) for sT in sTs]
    ms, alphas = consume(qi, sTs, mblks, ms, None)
    pv(load_v(qi), alphas)

    lam = (jnp.exp(jnp.sum(lq1_ref[...] * lk1_ref[...], axis=-1, keepdims=True))
           - jnp.exp(jnp.sum(lq2_ref[...] * lk2_ref[...], axis=-1, keepdims=True))
           + LAMBDA_INIT)
    for g in range(ATT_G):
        acc = acc_ref[g, :, 0:2 * tq]
        inv_l = 1.0 / acc[V_HEAD_DIM:V_HEAD_DIM + 1, :]
        num = acc[:V_HEAD_DIM, :] * inv_l
        oT = num[:, :tq] - lam * num[:, tq:]
        o = _rms(oT.T, g_ref[...]) * (1.0 - LAMBDA_INIT)
        o_ref[:, g * 128:(g + 1) * 128] = o.astype(BF16)


def _attention(cparts, qT, k, vT, lq1, lk1, lq2, lk2, subln_g):
    S = k.shape[0]
    nq, nk = S // ATT_TQ, S // ATT_TK
    G = ATT_G
    vec = lambda h, i: (0, 0)
    return pl.pallas_call(
        _attn_kernel,
        grid=(N_HEADS // G, nq),
        in_specs=[
            pl.BlockSpec(memory_space=pltpu.SMEM),
            pl.BlockSpec((G, None, 128, ATT_TQ), lambda h, i: (h, i, 0, 0)),
            pl.BlockSpec((S, G * 128), lambda h, i: (0, h)),
            pl.BlockSpec((G, nk, V_ROWS, ATT_TK), lambda h, i: (h, 0, 0, 0)),
            pl.BlockSpec((1, QK_HEAD_DIM), vec),
            pl.BlockSpec((1, QK_HEAD_DIM), vec),
            pl.BlockSpec((1, QK_HEAD_DIM), vec),
            pl.BlockSpec((1, QK_HEAD_DIM), vec),
            pl.BlockSpec((1, V_HEAD_DIM), vec),
        ],
        out_specs=pl.BlockSpec((ATT_TQ, G * 128), lambda h, i: (i, h)),
        out_shape=jax.ShapeDtypeStruct((S, ATT_WIDTH), BF16),
        scratch_shapes=[
            pltpu.VMEM((G, 256, 2 * ATT_TQ), BF16),
            pltpu.VMEM((G, V_ROWS, ATT_PADW), F32),
            pltpu.VMEM((G, ATT_TK, ATT_PADW), BF16),
            pltpu.VMEM((ATT_TK, 128), BF16),
            pltpu.VMEM((2, G, ATT_TK, ATT_PADW), F32),
        ],
        compiler_params=_params(("arbitrary", "arbitrary")),
        name="diff_attn",
    )(cparts, qT, k, vT, lq1, lk1, lq2, lk2, subln_g)


def _mix_kernel(o_ref, u_ref, uprev_ref, gate_ref, x_ref, wg_ref, ps_ref, wa_ref, wp_ref, wo_ref,
                gm_ref, x1_ref, hm_ref, ext_ref, *, bm):
    i = pl.program_id(0)
    u = u_ref[...]
    halo = uprev_ref[...]
    ext_ref[0:POOL_HALO, :] = jnp.where(i == 0, jnp.zeros_like(halo), halo)
    ext_ref[POOL_HALO:POOL_HALO + bm, :] = u
    t = i * bm + lax.broadcasted_iota(jnp.int32, (bm, 1), 0)
    pms = []
    for g, w in enumerate(POOL_WINDOWS):
        c0, c1 = g * POOL_GROUP_WIDTH, (g + 1) * POOL_GROUP_WIDTH
        ug = u[:, c0:c1]
        win = ug
        for d in range(1, w):
            win = win + ext_ref[POOL_HALO - d:POOL_HALO - d + bm, c0:c1]
        count = jnp.minimum(t + 1, w).astype(F32)
        pooled = win / count - ug
        pms.append(_dot(pooled.astype(BF16), wg_ref[g]))
    pm = (jnp.concatenate(pms, axis=-1) * ps_ref[...]).astype(BF16)
    p_branch = _dot(pm, wp_ref[...])
    a_branch = _dot(o_ref[...], wa_ref[...])
    gates = gate_ref[...]
    merged = gates[:, :D_MODEL].astype(F32) * a_branch + gates[:, D_MODEL:].astype(F32) * p_branch
    x1 = x_ref[...] + _dot(merged.astype(BF16), wo_ref[...])
    x1_ref[...] = x1
    hm_ref[...] = _rms(x1, gm_ref[...]).astype(BF16)


def _mix(o, u, gates, x, wg, pool_scale, wa, wp, wo, g_mlp, bm=256):
    S = x.shape[0]
    const2 = lambda i: (0, 0)
    once = pl.Buffered(1)
    halo_blocks = bm // POOL_HALO
    return pl.pallas_call(
        functools.partial(_mix_kernel, bm=bm),
        grid=(S // bm,),
        in_specs=[
            pl.BlockSpec((bm, ATT_WIDTH), lambda i: (i, 0)),
            pl.BlockSpec((bm, POOL_WIDTH), lambda i: (i, 0)),
            pl.BlockSpec((POOL_HALO, POOL_WIDTH), lambda i: (jnp.maximum(i * halo_blocks - 1, 0), 0)),
            pl.BlockSpec((bm, 2 * D_MODEL), lambda i: (i, 0)),
            pl.BlockSpec((bm, D_MODEL), lambda i: (i, 0)),
            pl.BlockSpec((len(POOL_WINDOWS), POOL_GROUP_WIDTH, POOL_GROUP_WIDTH), lambda i: (0, 0, 0),
                         pipeline_mode=once),
            pl.BlockSpec((1, POOL_WIDTH), const2),
            pl.BlockSpec((ATT_WIDTH, D_MODEL), const2, pipeline_mode=once),
            pl.BlockSpec((POOL_WIDTH, D_MODEL), const2, pipeline_mode=once),
            pl.BlockSpec((D_MODEL, D_MODEL), const2, pipeline_mode=once),
            pl.BlockSpec((1, D_MODEL), const2),
        ],
        out_specs=[
            pl.BlockSpec((bm, D_MODEL), lambda i: (i, 0)),
            pl.BlockSpec((bm, D_MODEL), lambda i: (i, 0)),
        ],
        out_shape=[
            jax.ShapeDtypeStruct((S, D_MODEL), F32),
            jax.ShapeDtypeStruct((S, D_MODEL), BF16),
        ],
        scratch_shapes=[pltpu.VMEM((POOL_HALO + bm, POOL_WIDTH), F32)],
        compiler_params=_params(("arbitrary",)),
        name="mix_out",
    )(o, u, u, gates, x, wg, pool_scale, wa, wp, wo, g_mlp)


def _mlp_kernel(hm_ref, x1_ref, wu_ref, wd_ref, gp_ref, x2_ref, hp_ref, acc_ref):
    f = pl.program_id(1)

    @pl.when(f == 0)
    def _():
        acc_ref[...] = x1_ref[...]

    a = jnp.maximum(_dot(hm_ref[...], wu_ref[...]), 0.0)
    acc_ref[...] += _dot((a * a).astype(BF16), wd_ref[...])

    @pl.when(f == pl.num_programs(1) - 1)
    def _():
        x2 = acc_ref[...]
        x2_ref[...] = x2
        hp_ref[...] = _rms(x2, gp_ref[...]).astype(BF16)


def _mlp(hm, x1, wu, wd, g_ple, bm=512, bf=1024):
    S = hm.shape[0]
    return pl.pallas_call(
        _mlp_kernel,
        grid=(S // bm, D_FF // bf),
        in_specs=[
            pl.BlockSpec((bm, D_MODEL), lambda i, f: (i, 0)),
            pl.BlockSpec((bm, D_MODEL), lambda i, f: (i, 0)),
            pl.BlockSpec((D_MODEL, bf), lambda i, f: (0, f)),
            pl.BlockSpec((bf, D_MODEL), lambda i, f: (f, 0)),
            pl.BlockSpec((1, D_MODEL), lambda i, f: (0, 0)),
        ],
        out_specs=[
            pl.BlockSpec((bm, D_MODEL), lambda i, f: (i, 0)),
            pl.BlockSpec((bm, D_MODEL), lambda i, f: (i, 0)),
        ],
        out_shape=[
            jax.ShapeDtypeStruct((S, D_MODEL), F32),
            jax.ShapeDtypeStruct((S, D_MODEL), BF16),
        ],
        scratch_shapes=[pltpu.VMEM((bm, D_MODEL), F32)],
        compiler_params=_params(("arbitrary", "arbitrary")),
        name="mlp",
    )(hm, x1, wu, wd, g_ple)


def _ple_kernel(hp_ref, x2_ref, p_ref, wgate_ref, wple_ref, gf_ref, out_ref):
    gate = jax.nn.sigmoid(_dot(hp_ref[...], wgate_ref[...]))
    emb = _dot(p_ref[...].astype(BF16), wple_ref[...])
    x3 = x2_ref[...] + emb * gate
    out_ref[...] = _rms(x3, gf_ref[...])


def _ple(hp, x2, p, wgate, wple, g_final, bm=512):
    S = hp.shape[0]
    const2 = lambda i: (0, 0)
    return pl.pallas_call(
        _ple_kernel,
        grid=(S // bm,),
        in_specs=[
            pl.BlockSpec((bm, D_MODEL), lambda i: (i, 0)),
            pl.BlockSpec((bm, D_MODEL), lambda i: (i, 0)),
            pl.BlockSpec((bm, PLE_DIM), lambda i: (i, 0)),
            pl.BlockSpec((D_MODEL, D_MODEL), const2),
            pl.BlockSpec((PLE_DIM, D_MODEL), const2),
            pl.BlockSpec((1, D_MODEL), const2),
        ],
        out_specs=pl.BlockSpec((bm, D_MODEL), lambda i: (i, 0)),
        out_shape=jax.ShapeDtypeStruct((S, D_MODEL), F32),
        compiler_params=_params(("arbitrary",)),
        name="ple_final",
    )(hp, x2, p, wgate, wple, g_final)


def kernel(x, p, norm_mix_g, w_in, lambda_q1, lambda_k1, lambda_q2, lambda_k2, subln_g, pool_grp_w, pool_scale, w_attn_br, w_pool_br, w_out, norm_mlp_g, w_mlp_up, w_mlp_down, norm_ple_g, w_ple, w_ple_gate, final_norm_g):
    B, S, D = x.shape
    assert (B, S, D) == (1, SEQ, D_MODEL) and norm_mix_g.shape[0] == 1
    cparts = jnp.asarray(_alibi_parts())

    x2d = x[0]
    h = _prenorm(x2d, norm_mix_g)
    qT, k, vT, u, gates = _inproj(h, w_in[0])
    o = _attention(cparts, qT, k, vT, lambda_q1, lambda_k1, lambda_q2, lambda_k2, subln_g)
    x1, hm = _mix(o, u, gates, x2d, pool_grp_w[0].astype(BF16), pool_scale,
                  w_attn_br[0].astype(BF16), w_pool_br[0].astype(BF16), w_out[0].astype(BF16),
                  norm_mlp_g)
    x2, hp = _mlp(hm, x1, w_mlp_up[0].astype(BF16), w_mlp_down[0].astype(BF16), norm_ple_g)
    out = _ple(hp, x2, p[0, 0], w_ple_gate[0].astype(BF16), w_ple[0].astype(BF16),
               final_norm_g.reshape(1, D_MODEL))
    return out[None]
```

```python
import functools
import math

import numpy as np
import jax
import jax.numpy as jnp
from jax import lax
from jax.experimental import pallas as pl
from jax.experimental.pallas import tpu as pltpu

F32 = jnp.float32
BF16 = jnp.bfloat16

D_MODEL = 2048
SEQ = 8192
N_HEADS = 8
V_HEAD_DIM = 128
QK_HEAD_DIM = 64
ATT_WIDTH = N_HEADS * V_HEAD_DIM
POOL_WIDTH = 1024
POOL_WINDOWS = (2, 4, 8, 16)
POOL_GROUP_WIDTH = POOL_WIDTH // len(POOL_WINDOWS)
POOL_HALO = 16
D_FF = 4 * D_MODEL
PLE_DIM = 256
NORM_EPS = 1e-6
LAMBDA_INIT = 0.8 - 0.6 * math.exp(-0.3 * 0)
NEG_BIG = -1e30
LOG2E = math.log2(math.e)
V_ROWS = V_HEAD_DIM + 16
BIAS_TERMS = 3


def _alibi_parts():
    import ml_dtypes
    rest = 2.0 ** (-8.0 * np.arange(1, N_HEADS + 1) / N_HEADS) * LOG2E
    parts = []
    for _ in range(BIAS_TERMS):
        piece = rest.astype(ml_dtypes.bfloat16).astype(np.float64)
        parts.append(piece)
        rest = rest - piece
    return np.stack(parts, axis=1).reshape(-1).astype(np.float32)

ATT_TQ = 256
ATT_TK = 512
ATT_G = 4
ATT_PADW = 2 * ATT_TQ
VMEM_LIMIT = 56 * 1024 * 1024
INPROJ_VMEM_LIMIT = 60 * 1024 * 1024


def _rms(xf, g):
    return xf * lax.rsqrt(jnp.mean(xf * xf, axis=-1, keepdims=True) + NORM_EPS) * g


def _dot(a, b):
    return jnp.dot(a, b, preferred_element_type=F32)


def _params(sem, vmem_limit=VMEM_LIMIT):
    return pltpu.CompilerParams(dimension_semantics=sem, vmem_limit_bytes=vmem_limit)


def _prenorm_kernel(x_ref, g_ref, h_ref):
    h_ref[...] = _rms(x_ref[...], g_ref[...]).astype(BF16)


def _prenorm(x, g, bm=512):
    S = x.shape[0]
    return pl.pallas_call(
        _prenorm_kernel,
        grid=(S // bm,),
        in_specs=[pl.BlockSpec((bm, D_MODEL), lambda i: (i, 0)),
                  pl.BlockSpec((1, D_MODEL), lambda i: (0, 0))],
        out_specs=pl.BlockSpec((bm, D_MODEL), lambda i: (i, 0)),
        out_shape=jax.ShapeDtypeStruct((S, D_MODEL), BF16),
        compiler_params=_params(("arbitrary",)),
        name="prenorm",
    )(x, g)


IN_PHASE_COLS = 1024
IN_CHUNK = 256
N_IN_PHASES = 8
GATE_PHASE0 = 4


def _inproj_kernel(h_ref, w_ref, qT_ref, k_ref, vT_ref, u_ref, gate_ref, wb_ref, *, bm):
    j = pl.program_id(0)
    i = pl.program_id(1)
    half = IN_PHASE_COLS // 2

    @pl.when(jnp.logical_and(i == 0, j != 1))
    def _():
        wb_ref[...] = w_ref[...].astype(BF16)

    @pl.when(jnp.logical_and(i == 0, j == 1))
    def _():
        for hh in range(N_HEADS):
            for m in range(2):
                src = m * half + hh * QK_HEAD_DIM
                dst = hh * 128 + m * QK_HEAD_DIM
                wb_ref[:, dst:dst + QK_HEAD_DIM] = w_ref[:, src:src + QK_HEAD_DIM].astype(BF16)

    def chunks():
        h = h_ref[...]
        for c in range(IN_PHASE_COLS // IN_CHUNK):
            yield c, _dot(h, wb_ref[:, c * IN_CHUNK:(c + 1) * IN_CHUNK])

    @pl.when(j == 0)
    def _():
        heads_per_chunk = IN_CHUNK // QK_HEAD_DIM
        for c, z in chunks():
            zT = (z * (QK_HEAD_DIM ** -0.5 * LOG2E)).T.astype(BF16)
            m = (c * IN_CHUNK) // half
            h0 = ((c * IN_CHUNK) % half) // QK_HEAD_DIM
            for hl in range(heads_per_chunk):
                for r in range(bm // ATT_TQ):
                    qT_ref[h0 + hl, r, m * QK_HEAD_DIM:(m + 1) * QK_HEAD_DIM, :] = (
                        zT[hl * QK_HEAD_DIM:(hl + 1) * QK_HEAD_DIM, r * ATT_TQ:(r + 1) * ATT_TQ])

    @pl.when(j == 1)
    def _():
        for c, z in chunks():
            k_ref[:, c * IN_CHUNK:(c + 1) * IN_CHUNK] = z.astype(BF16)

    @pl.when(j == 2)
    def _():
        heads_per_chunk = IN_CHUNK // V_HEAD_DIM
        ones = jnp.ones((V_ROWS - V_HEAD_DIM, ATT_TK), BF16)
        for c, z in chunks():
            zT = z.T.astype(BF16)
            for hl in range(heads_per_chunk):
                hh = c * heads_per_chunk + hl
                for r in range(bm // ATT_TK):
                    vT_ref[hh, r, 0:V_HEAD_DIM, :] = (
                        zT[hl * V_HEAD_DIM:(hl + 1) * V_HEAD_DIM, r * ATT_TK:(r + 1) * ATT_TK])
                    vT_ref[hh, r, V_HEAD_DIM:V_ROWS, :] = ones

    @pl.when(j == 3)
    def _():
        for c, z in chunks():
            u_ref[:, c * IN_CHUNK:(c + 1) * IN_CHUNK] = z

    @pl.when(j >= GATE_PHASE0)
    def _():
        for c, z in chunks():
            gate_ref[:, c * IN_CHUNK:(c + 1) * IN_CHUNK] = jax.nn.sigmoid(z).astype(BF16)


def _inproj(h, w_in, bm=1024):
    S = h.shape[0]
    n_rows = S // bm
    nq, nk = S // ATT_TQ, S // ATT_TK

    def rows_in_phase(phase):
        def f(j, i):
            return jnp.where(j == phase, i, jnp.where(j < phase, 0, n_rows - 1))
        return f

    rq, rk, rv, ru = (rows_in_phase(ph) for ph in range(4))
    return pl.pallas_call(
        functools.partial(_inproj_kernel, bm=bm),
        grid=(N_IN_PHASES, n_rows),
        in_specs=[
            pl.BlockSpec((bm, D_MODEL), lambda j, i: (i, 0)),
            pl.BlockSpec((D_MODEL, IN_PHASE_COLS), lambda j, i: (0, j)),
        ],
        out_specs=[
            pl.BlockSpec((N_HEADS, bm // ATT_TQ, 128, ATT_TQ), lambda j, i: (0, rq(j, i), 0, 0)),
            pl.BlockSpec((bm, 1024), lambda j, i: (rk(j, i), 0)),
            pl.BlockSpec((N_HEADS, bm // ATT_TK, V_ROWS, ATT_TK), lambda j, i: (0, rv(j, i), 0, 0)),
            pl.BlockSpec((bm, POOL_WIDTH), lambda j, i: (ru(j, i), 0)),
            pl.BlockSpec((bm, IN_PHASE_COLS),
                         lambda j, i: (jnp.where(j >= GATE_PHASE0, i, 0), jnp.maximum(j - GATE_PHASE0, 0))),
        ],
        out_shape=[
            jax.ShapeDtypeStruct((N_HEADS, nq, 128, ATT_TQ), BF16),
            jax.ShapeDtypeStruct((S, 1024), BF16),
            jax.ShapeDtypeStruct((N_HEADS, nk, V_ROWS, ATT_TK), BF16),
            jax.ShapeDtypeStruct((S, POOL_WIDTH), F32),
            jax.ShapeDtypeStruct((S, 2 * D_MODEL), BF16),
        ],
        scratch_shapes=[pltpu.VMEM((D_MODEL, IN_PHASE_COLS), BF16)],
        compiler_params=_params(("arbitrary", "arbitrary"), vmem_limit=INPROJ_VMEM_LIMIT),
        name="inproj",
    )(h, w_in)


def _attn_kernel(cpart_ref, qT_ref, k_ref, vT_ref, lq1_ref, lk1_ref, lq2_ref, lk2_ref, g_ref,
                 o_ref, qs_ref, acc_ref, p_ref, feat_ref, s_ref):
    hg = pl.program_id(0)
    qi = pl.program_id(1)
    tq, tk = ATT_TQ, ATT_TK

    krow = lax.broadcasted_iota(jnp.int32, (tk, 128), 0)
    klane = lax.broadcasted_iota(jnp.int32, (tk, 128), 1)
    feat = jnp.where(klane < BIAS_TERMS, krow >> 4, jnp.where(klane < 2 * BIAS_TERMS, krow & 15, 0))
    feat_ref[...] = feat.astype(F32).astype(BF16)

    row = lax.broadcasted_iota(jnp.int32, (128, tq), 0)
    brow = lax.broadcasted_iota(jnp.int32, (128, 2 * tq), 0)
    slopes = []
    for g in range(ATT_G):
        qT = qT_ref[g]
        zero = jnp.zeros_like(qT)
        qs_ref[g, 0:128, :tq] = jnp.where(row < QK_HEAD_DIM, qT, zero)
        qs_ref[g, 0:128, tq:] = jnp.where(row >= QK_HEAD_DIM, qT, zero)
        parts = [cpart_ref[(hg * ATT_G + g) * BIAS_TERMS + t] for t in range(BIAS_TERMS)]
        bias_rows = jnp.zeros((128, 2 * tq), F32)
        for t in range(BIAS_TERMS):
            bias_rows = jnp.where(brow == t, 16.0 * parts[t], bias_rows)
            bias_rows = jnp.where(brow == BIAS_TERMS + t, parts[t], bias_rows)
        qs_ref[g, 128:256, :] = bias_rows.astype(BF16)
        slopes.append(sum(parts[1:], parts[0]))
    acc_ref[...] = jnp.zeros_like(acc_ref)
    p_ref[...] = jnp.zeros_like(p_ref)

    def load_k(j):
        start = pl.multiple_of(j * tk, tk)
        return [k_ref[pl.ds(start, tk), g * 128:(g + 1) * 128] for g in range(ATT_G)]

    def load_v(j):
        return [vT_ref[g, j] for g in range(ATT_G)]

    def produce(k_blks, par):
        mblks = []
        for g in range(ATT_G):
            k_aug = jnp.concatenate([k_blks[g], feat_ref[...]], axis=1)
            sT = _dot(k_aug, qs_ref[g])
            s_ref[par, g, :, 0:2 * tq] = sT
            mblks.append(jnp.max(sT, axis=0, keepdims=True))
        return tuple(mblks)

    def pv(v_blks, alphas):
        for g in range(ATT_G):
            acc_ref[g, :, 0:2 * tq] = (alphas[g] * acc_ref[g, :, 0:2 * tq]
                                       + _dot(v_blks[g], p_ref[g, :, 0:2 * tq]))

    def consume(j, sTs, mblks, ms, valid):
        new_ms, new_alphas = [], []
        for g in range(ATT_G):
            off = slopes[g] * (j * tk - qi * tq).astype(F32)
            cand = mblks[g] + off
            if valid is not None:
                cand = jnp.where(valid, cand, NEG_BIG)
            m_new = jnp.maximum(ms[g], cand)
            shift = m_new - off
            if valid is not None:
                shift = jnp.where(valid, shift, -NEG_BIG)
            new_alphas.append(jnp.exp2(ms[g] - m_new))
            new_ms.append(m_new)
            p_ref[g, :, 0:2 * tq] = jnp.exp2(sTs[g] - shift).astype(BF16)
        return tuple(new_ms), tuple(new_alphas)

    def half_step(j, par, ms, alphas, mblks, valid):
        k_blks = load_k(jnp.minimum(j + 1, vT_ref.shape[1] - 1))
        v_blks = load_v(jnp.maximum(j - 1, 0))
        mblks_next = produce(k_blks, 1 - par)
        pv(v_blks, alphas)
        ms, alphas = consume(j, [s_ref[par, g, :, 0:2 * tq] for g in range(ATT_G)], mblks, ms, valid)
        return ms, alphas, mblks_next

    n_full = (qi * tq) // tk

    def body(t, carry):
        carry = half_step(2 * t, 0, *carry, None)
        return half_step(2 * t + 1, 1, *carry, 2 * t + 1 < n_full)

    mblks0 = produce(load_k(0), 0)
    init = (tuple(jnp.full((1, 2 * tq), NEG_BIG, F32) for _ in range(ATT_G)),
            tuple(jnp.ones((1, 2 * tq), F32) for _ in range(ATT_G)),
            mblks0)
    ms, alphas, _ = lax.fori_loop(0, (n_full + 1) // 2, body, init)

    pv(load_v(jnp.maximum(n_full - 1, 0)), alphas)
    kpos = lax.broadcasted_iota(jnp.int32, (tk, 2 * tq), 0)
    qcol = lax.broadcasted_iota(jnp.int32, (tk, 2 * tq), 1)
    qpos = jnp.where(qcol >= tq, qcol - tq, qcol) + (qi * tq - n_full * tk)
    causal = kpos <= qpos
    par_last = n_full % 2
    sTs = [jnp.where(causal, s_ref[par_last, g, :, 0:2 * tq], NEG_BIG) for g in range(ATT_G)]
    mblks = [jnp.max(sT, axis=0, keepdims=True) for sT in sTs]
    ms, alphas = consume(n_full, sTs, mblks, ms, None)
    pv(load_v(n_full), alphas)

    lam = (jnp.exp(jnp.sum(lq1_ref[...] * lk1_ref[...], axis=-1, keepdims=True))
           - jnp.exp(jnp.sum(lq2_ref[...] * lk2_ref[...], axis=-1, keepdims=True))
           + LAMBDA_INIT)
    for g in range(ATT_G):
        acc = acc_ref[g, :, 0:2 * tq]
        inv_l = 1.0 / acc[V_HEAD_DIM:V_HEAD_DIM + 1, :]
        num = acc[:V_HEAD_DIM, :] * inv_l
        oT = num[:, :tq] - lam * num[:, tq:]
        o = _rms(oT.T, g_ref[...]) * (1.0 - LAMBDA_INIT)
        o_ref[:, g * 128:(g + 1) * 128] = o.astype(BF16)


def _attention(cparts, qT, k, vT, lq1, lk1, lq2, lk2, subln_g):
    S = k.shape[0]
    nq, nk = S // ATT_TQ, S // ATT_TK
    G = ATT_G
    vec = lambda h, i: (0, 0)
    return pl.pallas_call(
        _attn_kernel,
        grid=(N_HEADS // G, nq),
        in_specs=[
            pl.BlockSpec(memory_space=pltpu.SMEM),
            pl.BlockSpec((G, None, 128, ATT_TQ), lambda h, i: (h, i, 0, 0)),
            pl.BlockSpec((S, G * 128), lambda h, i: (0, h)),
            pl.BlockSpec((G, nk, V_ROWS, ATT_TK), lambda h, i: (h, 0, 0, 0)),
            pl.BlockSpec((1, QK_HEAD_DIM), vec),
            pl.BlockSpec((1, QK_HEAD_DIM), vec),
            pl.BlockSpec((1, QK_HEAD_DIM), vec),
            pl.BlockSpec((1, QK_HEAD_DIM), vec),
            pl.BlockSpec((1, V_HEAD_DIM), vec),
        ],
        out_specs=pl.BlockSpec((ATT_TQ, G * 128), lambda h, i: (i, h)),
        out_shape=jax.ShapeDtypeStruct((S, ATT_WIDTH), BF16),
        scratch_shapes=[
            pltpu.VMEM((G, 256, 2 * ATT_TQ), BF16),
            pltpu.VMEM((G, V_ROWS, ATT_PADW), F32),
            pltpu.VMEM((G, ATT_TK, ATT_PADW), BF16),
            pltpu.VMEM((ATT_TK, 128), BF16),
            pltpu.VMEM((2, G, ATT_TK, ATT_PADW), F32),
        ],
        compiler_params=_params(("arbitrary", "arbitrary")),
        name="diff_attn",
    )(cparts, qT, k, vT, lq1, lk1, lq2, lk2, subln_g)


def _mix_kernel(o_ref, u_ref, uprev_ref, gate_ref, x_ref, wg_ref, ps_ref, wa_ref, wp_ref, wo_ref,
                gm_ref, x1_ref, hm_ref, ext_ref, *, bm):
    i = pl.program_id(0)
    u = u_ref[...]
    halo = uprev_ref[...]
    ext_ref[0:POOL_HALO, :] = jnp.where(i == 0, jnp.zeros_like(halo), halo)
    ext_ref[POOL_HALO:POOL_HALO + bm, :] = u
    t = i * bm + lax.broadcasted_iota(jnp.int32, (bm, 1), 0)
    pms = []
    for g, w in enumerate(POOL_WINDOWS):
        c0, c1 = g * POOL_GROUP_WIDTH, (g + 1) * POOL_GROUP_WIDTH
        ug = u[:, c0:c1]
        win = ug
        for d in range(1, w):
            win = win + ext_ref[POOL_HALO - d:POOL_HALO - d + bm, c0:c1]
        count = jnp.minimum(t + 1, w).astype(F32)
        pooled = win / count - ug
        pms.append(_dot(pooled.astype(BF16), wg_ref[g]))
    pm = (jnp.concatenate(pms, axis=-1) * ps_ref[...]).astype(BF16)
    p_branch = _dot(pm, wp_ref[...])
    a_branch = _dot(o_ref[...], wa_ref[...])
    gates = gate_ref[...]
    merged = gates[:, :D_MODEL].astype(F32) * a_branch + gates[:, D_MODEL:].astype(F32) * p_branch
    x1 = x_ref[...] + _dot(merged.astype(BF16), wo_ref[...])
    x1_ref[...] = x1
    hm_ref[...] = _rms(x1, gm_ref[...]).astype(BF16)


def _mix(o, u, gates, x, wg, pool_scale, wa, wp, wo, g_mlp, bm=256):
    S = x.shape[0]
    const2 = lambda i: (0, 0)
    once = pl.Buffered(1)
    halo_blocks = bm // POOL_HALO
    return pl.pallas_call(
        functools.partial(_mix_kernel, bm=bm),
        grid=(S // bm,),
        in_specs=[
            pl.BlockSpec((bm, ATT_WIDTH), lambda i: (i, 0)),
            pl.BlockSpec((bm, POOL_WIDTH), lambda i: (i, 0)),
            pl.BlockSpec((POOL_HALO, POOL_WIDTH), lambda i: (jnp.maximum(i * halo_blocks - 1, 0), 0)),
            pl.BlockSpec((bm, 2 * D_MODEL), lambda i: (i, 0)),
            pl.BlockSpec((bm, D_MODEL), lambda i: (i, 0)),
            pl.BlockSpec((len(POOL_WINDOWS), POOL_GROUP_WIDTH, POOL_GROUP_WIDTH), lambda i: (0, 0, 0),
                         pipeline_mode=once),
            pl.BlockSpec((1, POOL_WIDTH), const2),
            pl.BlockSpec((ATT_WIDTH, D_MODEL), const2, pipeline_mode=once),
            pl.BlockSpec((POOL_WIDTH, D_MODEL), const2, pipeline_mode=once),
            pl.BlockSpec((D_MODEL, D_MODEL), const2, pipeline_mode=once),
            pl.BlockSpec((1, D_MODEL), const2),
        ],
        out_specs=[
            pl.BlockSpec((bm, D_MODEL), lambda i: (i, 0)),
            pl.BlockSpec((bm, D_MODEL), lambda i: (i, 0)),
        ],
        out_shape=[
            jax.ShapeDtypeStruct((S, D_MODEL), F32),
            jax.ShapeDtypeStruct((S, D_MODEL), BF16),
        ],
        scratch_shapes=[pltpu.VMEM((POOL_HALO + bm, POOL_WIDTH), F32)],
        compiler_params=_params(("arbitrary",)),
        name="mix_out",
    )(o, u, u, gates, x, wg, pool_scale, wa, wp, wo, g_mlp)


def _mlp_kernel(hm_ref, x1_ref, wu_ref, wd_ref, gp_ref, x2_ref, hp_ref, acc_ref):
    f = pl.program_id(1)

    @pl.when(f == 0)
    def _():
        acc_ref[...] = x1_ref[...]

    a = jnp.maximum(_dot(hm_ref[...], wu_ref[...]), 0.0)
    acc_ref[...] += _dot((a * a).astype(BF16), wd_ref[...])

    @pl.when(f == pl.num_programs(1) - 1)
    def _():
        x2 = acc_ref[...]
        x2_ref[...] = x2
        hp_ref[...] = _rms(x2, gp_ref[...]).astype(BF16)


def _mlp(hm, x1, wu, wd, g_ple, bm=512, bf=1024):
    S = hm.shape[0]
    return pl.pallas_call(
        _mlp_kernel,
        grid=(S // bm, D_FF // bf),
        in_specs=[
            pl.BlockSpec((bm, D_MODEL), lambda i, f: (i, 0)),
            pl.BlockSpec((bm, D_MODEL), lambda i, f: (i, 0)),
            pl.BlockSpec((D_MODEL, bf), lambda i, f: (0, f)),
            pl.BlockSpec((bf, D_MODEL), lambda i, f: (f, 0)),
            pl.BlockSpec((1, D_MODEL), lambda i, f: (0, 0)),
        ],
        out_specs=[
            pl.BlockSpec((bm, D_MODEL), lambda i, f: (i, 0)),
            pl.BlockSpec((bm, D_MODEL), lambda i, f: (i, 0)),
        ],
        out_shape=[
            jax.ShapeDtypeStruct((S, D_MODEL), F32),
            jax.ShapeDtypeStruct((S, D_MODEL), BF16),
        ],
        scratch_shapes=[pltpu.VMEM((bm, D_MODEL), F32)],
        compiler_params=_params(("arbitrary", "arbitrary")),
        name="mlp",
    )(hm, x1, wu, wd, g_ple)


def _ple_kernel(hp_ref, x2_ref, p_ref, wgate_ref, wple_ref, gf_ref, out_ref):
    gate = jax.nn.sigmoid(_dot(hp_ref[...], wgate_ref[...]))
    emb = _dot(p_ref[...].astype(BF16), wple_ref[...])
    x3 = x2_ref[...] + emb * gate
    out_ref[...] = _rms(x3, gf_ref[...])


def _ple(hp, x2, p, wgate, wple, g_final, bm=512):
    S = hp.shape[0]
    const2 = lambda i: (0, 0)
    return pl.pallas_call(
        _ple_kernel,
        grid=(S // bm,),
        in_specs=[
            pl.BlockSpec((bm, D_MODEL), lambda i: (i, 0)),
            pl.BlockSpec((bm, D_MODEL), lambda i: (i, 0)),
            pl.BlockSpec((bm, PLE_DIM), lambda i: (i, 0)),
            pl.BlockSpec((D_MODEL, D_MODEL), const2),
            pl.BlockSpec((PLE_DIM, D_MODEL), const2),
            pl.BlockSpec((1, D_MODEL), const2),
        ],
        out_specs=pl.BlockSpec((bm, D_MODEL), lambda i: (i, 0)),
        out_shape=jax.ShapeDtypeStruct((S, D_MODEL), F32),
        compiler_params=_params(("arbitrary",)),
        name="ple_final",
    )(hp, x2, p, wgate, wple, g_final)


def kernel(x, p, norm_mix_g, w_in, lambda_q1, lambda_k1, lambda_q2, lambda_k2, subln_g, pool_grp_w, pool_scale, w_attn_br, w_pool_br, w_out, norm_mlp_g, w_mlp_up, w_mlp_down, norm_ple_g, w_ple, w_ple_gate, final_norm_g):
    B, S, D = x.shape
    assert (B, S, D) == (1, SEQ, D_MODEL) and norm_mix_g.shape[0] == 1
    cparts = jnp.asarray(_alibi_parts())

    x2d = x[0]
    h = _prenorm(x2d, norm_mix_g)
    qT, k, vT, u, gates = _inproj(h, w_in[0])
    o = _attention(cparts, qT, k, vT, lambda_q1, lambda_k1, lambda_q2, lambda_k2, subln_g)
    x1, hm = _mix(o, u, gates, x2d, pool_grp_w[0].astype(BF16), pool_scale,
                  w_attn_br[0].astype(BF16), w_pool_br[0].astype(BF16), w_out[0].astype(BF16),
                  norm_mlp_g)
    x2, hp = _mlp(hm, x1, w_mlp_up[0].astype(BF16), w_mlp_down[0].astype(BF16), norm_ple_g)
    out = _ple(hp, x2, p[0, 0], w_ple_gate[0].astype(BF16), w_ple[0].astype(BF16),
               final_norm_g.reshape(1, D_MODEL))
    return out[None]
```

```python
import functools
import math

import numpy as np
import jax
import jax.numpy as jnp
from jax import lax
from jax.experimental import pallas as pl
from jax.experimental.pallas import tpu as pltpu

F32 = jnp.float32
BF16 = jnp.bfloat16

D_MODEL = 2048
SEQ = 8192
N_HEADS = 8
V_HEAD_DIM = 128
QK_HEAD_DIM = 64
ATT_WIDTH = N_HEADS * V_HEAD_DIM
POOL_WIDTH = 1024
POOL_WINDOWS = (2, 4, 8, 16)
POOL_GROUP_WIDTH = POOL_WIDTH // len(POOL_WINDOWS)
POOL_HALO = 16
D_FF = 4 * D_MODEL
PLE_DIM = 256
NORM_EPS = 1e-6
LAMBDA_INIT = 0.8 - 0.6 * math.exp(-0.3 * 0)
NEG_BIG = -1e30
LOG2E = math.log2(math.e)
V_ROWS = V_HEAD_DIM + 16
BIAS_TERMS = 3


def _alibi_parts():
    import ml_dtypes
    rest = 2.0 ** (-8.0 * np.arange(1, N_HEADS + 1) / N_HEADS) * LOG2E
    parts = []
    for _ in range(BIAS_TERMS):
        piece = rest.astype(ml_dtypes.bfloat16).astype(np.float64)
        parts.append(piece)
        rest = rest - piece
    return np.stack(parts, axis=1).reshape(-1).astype(np.float32)

ATT_TQ = 256
ATT_TK = 512
ATT_G = 4
ATT_PADW = 2 * ATT_TQ
VMEM_LIMIT = 56 * 1024 * 1024
INPROJ_VMEM_LIMIT = 60 * 1024 * 1024


def _rms(xf, g):
    return xf * lax.rsqrt(jnp.mean(xf * xf, axis=-1, keepdims=True) + NORM_EPS) * g


def _dot(a, b):
    return jnp.dot(a, b, preferred_element_type=F32)


def _params(sem, vmem_limit=VMEM_LIMIT):
    return pltpu.CompilerParams(dimension_semantics=sem, vmem_limit_bytes=vmem_limit)


def _prenorm_kernel(x_ref, g_ref, h_ref):
    h_ref[...] = _rms(x_ref[...], g_ref[...]).astype(BF16)


def _prenorm(x, g, bm=512):
    S = x.shape[0]
    return pl.pallas_call(
        _prenorm_kernel,
        grid=(S // bm,),
        in_specs=[pl.BlockSpec((bm, D_MODEL), lambda i: (i, 0)),
                  pl.BlockSpec((1, D_MODEL), lambda i: (0, 0))],
        out_specs=pl.BlockSpec((bm, D_MODEL), lambda i: (i, 0)),
        out_shape=jax.ShapeDtypeStruct((S, D_MODEL), BF16),
        compiler_params=_params(("arbitrary",)),
        name="prenorm",
    )(x, g)


IN_PHASE_COLS = 1024
IN_CHUNK = 256
N_IN_PHASES = 8
GATE_PHASE0 = 4


def _inproj_kernel(h_ref, w_ref, qT_ref, k_ref, vT_ref, u_ref, gate_ref, wb_ref, *, bm):
    j = pl.program_id(0)
    i = pl.program_id(1)
    half = IN_PHASE_COLS // 2

    @pl.when(jnp.logical_and(i == 0, j != 1))
    def _():
        wb_ref[...] = w_ref[...].astype(BF16)

    @pl.when(jnp.logical_and(i == 0, j == 1))
    def _():
        for hh in range(N_HEADS):
            for m in range(2):
                src = m * half + hh * QK_HEAD_DIM
                dst = hh * 128 + m * QK_HEAD_DIM
                wb_ref[:, dst:dst + QK_HEAD_DIM] = w_ref[:, src:src + QK_HEAD_DIM].astype(BF16)

    def chunks():
        h = h_ref[...]
        for c in range(IN_PHASE_COLS // IN_CHUNK):
            yield c, _dot(h, wb_ref[:, c * IN_CHUNK:(c + 1) * IN_CHUNK])

    @pl.when(j == 0)
    def _():
        heads_per_chunk = IN_CHUNK // QK_HEAD_DIM
        for c, z in chunks():
            zT = (z * (QK_HEAD_DIM ** -0.5 * LOG2E)).T.astype(BF16)
            m = (c * IN_CHUNK) // half
            h0 = ((c * IN_CHUNK) % half) // QK_HEAD_DIM
            for hl in range(heads_per_chunk):
                for r in range(bm // ATT_TQ):
                    qT_ref[h0 + hl, r, m * QK_HEAD_DIM:(m + 1) * QK_HEAD_DIM, :] = (
                        zT[hl * QK_HEAD_DIM:(hl + 1) * QK_HEAD_DIM, r * ATT_TQ:(r + 1) * ATT_TQ])

    @pl.when(j == 1)
    def _():
        for c, z in chunks():
            k_ref[:, c * IN_CHUNK:(c + 1) * IN_CHUNK] = z.astype(BF16)

    @pl.when(j == 2)
    def _():
        heads_per_chunk = IN_CHUNK // V_HEAD_DIM
        ones = jnp.ones((V_ROWS - V_HEAD_DIM, ATT_TK), BF16)
        for c, z in chunks():
            zT = z.T.astype(BF16)
            for hl in range(heads_per_chunk):
                hh = c * heads_per_chunk + hl
                for r in range(bm // ATT_TK):
                    vT_ref[hh, r, 0:V_HEAD_DIM, :] = (
                        zT[hl * V_HEAD_DIM:(hl + 1) * V_HEAD_DIM, r * ATT_TK:(r + 1) * ATT_TK])
                    vT_ref[hh, r, V_HEAD_DIM:V_ROWS, :] = ones

    @pl.when(j == 3)
    def _():
        for c, z in chunks():
            u_ref[:, c * IN_CHUNK:(c + 1) * IN_CHUNK] = z

    @pl.when(j >= GATE_PHASE0)
    def _():
        for c, z in chunks():
            gate_ref[:, c * IN_CHUNK:(c + 1) * IN_CHUNK] = jax.nn.sigmoid(z).astype(BF16)


def _inproj(h, w_in, bm=1024):
    S = h.shape[0]
    n_rows = S // bm
    nq, nk = S // ATT_TQ, S // ATT_TK

    def rows_in_phase(phase):
        def f(j, i):
            return jnp.where(j == phase, i, jnp.where(j < phase, 0, n_rows - 1))
        return f

    rq, rk, rv, ru = (rows_in_phase(ph) for ph in range(4))
    return pl.pallas_call(
        functools.partial(_inproj_kernel, bm=bm),
        grid=(N_IN_PHASES, n_rows),
        in_specs=[
            pl.BlockSpec((bm, D_MODEL), lambda j, i: (i, 0)),
            pl.BlockSpec((D_MODEL, IN_PHASE_COLS), lambda j, i: (0, j)),
        ],
        out_specs=[
            pl.BlockSpec((N_HEADS, bm // ATT_TQ, 128, ATT_TQ), lambda j, i: (0, rq(j, i), 0, 0)),
            pl.BlockSpec((bm, 1024), lambda j, i: (rk(j, i), 0)),
            pl.BlockSpec((N_HEADS, bm // ATT_TK, V_ROWS, ATT_TK), lambda j, i: (0, rv(j, i), 0, 0)),
            pl.BlockSpec((bm, POOL_WIDTH), lambda j, i: (ru(j, i), 0)),
            pl.BlockSpec((bm, IN_PHASE_COLS),
                         lambda j, i: (jnp.where(j >= GATE_PHASE0, i, 0), jnp.maximum(j - GATE_PHASE0, 0))),
        ],
        out_shape=[
            jax.ShapeDtypeStruct((N_HEADS, nq, 128, ATT_TQ), BF16),
            jax.ShapeDtypeStruct((S, 1024), BF16),
            jax.ShapeDtypeStruct((N_HEADS, nk, V_ROWS, ATT_TK), BF16),
            jax.ShapeDtypeStruct((S, POOL_WIDTH), F32),
            jax.ShapeDtypeStruct((S, 2 * D_MODEL), BF16),
        ],
        scratch_shapes=[pltpu.VMEM((D_MODEL, IN_PHASE_COLS), BF16)],
        compiler_params=_params(("arbitrary", "arbitrary"), vmem_limit=INPROJ_VMEM_LIMIT),
        name="inproj",
    )(h, w_in)


def _attn_kernel(cpart_ref, qT_ref, k_ref, vT_ref, lq1_ref, lk1_ref, lq2_ref, lk2_ref, g_ref,
                 *rest, n_cast):
    cast_in, (o_ref, *cast_out) = rest[:n_cast], rest[n_cast:2 * n_cast + 1]
    qs_ref, acc_ref, p_ref, feat_ref, s_ref = rest[2 * n_cast + 1:]
    hg = pl.program_id(0)
    qi = pl.program_id(1)
    tq, tk = ATT_TQ, ATT_TK

    for w_ref, wb_ref in zip(cast_in, cast_out):
        wb_ref[...] = w_ref[...].astype(BF16)

    krow = lax.broadcasted_iota(jnp.int32, (tk, 128), 0)
    klane = lax.broadcasted_iota(jnp.int32, (tk, 128), 1)
    feat = jnp.where(klane < BIAS_TERMS, krow >> 4, jnp.where(klane < 2 * BIAS_TERMS, krow & 15, 0))
    feat_ref[...] = feat.astype(F32).astype(BF16)

    row = lax.broadcasted_iota(jnp.int32, (128, tq), 0)
    brow = lax.broadcasted_iota(jnp.int32, (128, 2 * tq), 0)
    slopes = []
    for g in range(ATT_G):
        qT = qT_ref[g]
        zero = jnp.zeros_like(qT)
        qs_ref[g, 0:128, :tq] = jnp.where(row < QK_HEAD_DIM, qT, zero)
        qs_ref[g, 0:128, tq:] = jnp.where(row >= QK_HEAD_DIM, qT, zero)
        parts = [cpart_ref[(hg * ATT_G + g) * BIAS_TERMS + t] for t in range(BIAS_TERMS)]
        bias_rows = jnp.zeros((128, 2 * tq), F32)
        for t in range(BIAS_TERMS):
            bias_rows = jnp.where(brow == t, 16.0 * parts[t], bias_rows)
            bias_rows = jnp.where(brow == BIAS_TERMS + t, parts[t], bias_rows)
        qs_ref[g, 128:256, :] = bias_rows.astype(BF16)
        slopes.append(sum(parts[1:], parts[0]))
    acc_ref[...] = jnp.zeros_like(acc_ref)
    p_ref[...] = jnp.zeros_like(p_ref)

    def load_k(j):
        start = pl.multiple_of(j * tk, tk)
        return [k_ref[pl.ds(start, tk), g * 128:(g + 1) * 128] for g in range(ATT_G)]

    def load_v(j):
        return [vT_ref[g, j] for g in range(ATT_G)]

    def produce(k_blks, par):
        mblks = []
        for g in range(ATT_G):
            k_aug = jnp.concatenate([k_blks[g], feat_ref[...]], axis=1)
            sT = _dot(k_aug, qs_ref[g])
            s_ref[par, g, :, 0:2 * tq] = sT
            mblks.append(jnp.max(sT, axis=0, keepdims=True))
        return tuple(mblks)

    def pv(v_blks, alphas):
        for g in range(ATT_G):
            acc_ref[g, :, 0:2 * tq] = (alphas[g] * acc_ref[g, :, 0:2 * tq]
                                       + _dot(v_blks[g], p_ref[g, :, 0:2 * tq]))

    def consume(j, sTs, mblks, ms, valid):
        new_ms, new_alphas = [], []
        for g in range(ATT_G):
            off = slopes[g] * (j * tk - qi * tq).astype(F32)
            cand = mblks[g] + off
            if valid is not None:
                cand = jnp.where(valid, cand, NEG_BIG)
            m_new = jnp.maximum(ms[g], cand)
            shift = m_new - off
            if valid is not None:
                shift = jnp.where(valid, shift, -NEG_BIG)
            new_alphas.append(jnp.exp2(ms[g] - m_new))
            new_ms.append(m_new)
            p_ref[g, :, 0:2 * tq] = jnp.exp2(sTs[g] - shift).astype(BF16)
        return tuple(new_ms), tuple(new_alphas)

    def half_step(j, par, ms, alphas, mblks, valid):
        k_blks = load_k(jnp.minimum(j + 1, vT_ref.shape[1] - 1))
        v_blks = load_v(jnp.maximum(j - 1, 0))
        mblks_next = produce(k_blks, 1 - par)
        pv(v_blks, alphas)
        ms, alphas = consume(j, [s_ref[par, g, :, 0:2 * tq] for g in range(ATT_G)], mblks, ms, valid)
        return ms, alphas, mblks_next

    n_full = (qi * tq) // tk

    def body(t, carry):
        carry = half_step(2 * t, 0, *carry, None)
        return half_step(2 * t + 1, 1, *carry, 2 * t + 1 < n_full)

    mblks0 = produce(load_k(0), 0)
    init = (tuple(jnp.full((1, 2 * tq), NEG_BIG, F32) for _ in range(ATT_G)),
            tuple(jnp.ones((1, 2 * tq), F32) for _ in range(ATT_G)),
            mblks0)
    ms, alphas, _ = lax.fori_loop(0, (n_full + 1) // 2, body, init)

    pv(load_v(jnp.maximum(n_full - 1, 0)), alphas)
    kpos = lax.broadcasted_iota(jnp.int32, (tk, 2 * tq), 0)
    qcol = lax.broadcasted_iota(jnp.int32, (tk, 2 * tq), 1)
    qpos = jnp.where(qcol >= tq, qcol - tq, qcol) + (qi * tq - n_full * tk)
    causal = kpos <= qpos
    par_last = n_full % 2
    sTs = [jnp.where(causal, s_ref[par_last, g, :, 0:2 * tq], NEG_BIG) for g in range(ATT_G)]
    mblks = [jnp.max(sT, axis=0, keepdims=True) for sT in sTs]
    ms, alphas = consume(n_full, sTs, mblks, ms, None)
    pv(load_v(n_full), alphas)

    lam = (jnp.exp(jnp.sum(lq1_ref[...] * lk1_ref[...], axis=-1, keepdims=True))
           - jnp.exp(jnp.sum(lq2_ref[...] * lk2_ref[...], axis=-1, keepdims=True))
           + LAMBDA_INIT)
    for g in range(ATT_G):
        acc = acc_ref[g, :, 0:2 * tq]
        inv_l = 1.0 / acc[V_HEAD_DIM:V_HEAD_DIM + 1, :]
        num = acc[:V_HEAD_DIM, :] * inv_l
        oT = num[:, :tq] - lam * num[:, tq:]
        o = _rms(oT.T, g_ref[...]) * (1.0 - LAMBDA_INIT)
        o_ref[:, g * 128:(g + 1) * 128] = o.astype(BF16)


def _cast_view(w, n_chunks):
    cols = w.shape[-1]
    rows = w.size // cols
    while rows % (n_chunks * 16):
        assert cols % 256 == 0, w.shape
        cols //= 2
        rows *= 2
    return w.reshape(rows, cols)


def _attention(cparts, qT, k, vT, lq1, lk1, lq2, lk2, subln_g, cast_ws):
    S = k.shape[0]
    nq, nk = S // ATT_TQ, S // ATT_TK
    G = ATT_G
    n_steps = (N_HEADS // G) * nq
    vec = lambda h, i: (0, 0)
    once = pl.Buffered(1)
    views = [_cast_view(w, n_steps) for w in cast_ws]
    cast_specs = [pl.BlockSpec((v.shape[0] // n_steps, v.shape[1]), lambda h, i: (h * nq + i, 0))
                  for v in views]
    outs = pl.pallas_call(
        functools.partial(_attn_kernel, n_cast=len(views)),
        grid=(N_HEADS // G, nq),
        in_specs=[
            pl.BlockSpec(memory_space=pltpu.SMEM),
            pl.BlockSpec((G, None, 128, ATT_TQ), lambda h, i: (h, i, 0, 0)),
            pl.BlockSpec((S, G * 128), lambda h, i: (0, h), pipeline_mode=once),
            pl.BlockSpec((G, nk, V_ROWS, ATT_TK), lambda h, i: (h, 0, 0, 0), pipeline_mode=once),
            pl.BlockSpec((1, QK_HEAD_DIM), vec),
            pl.BlockSpec((1, QK_HEAD_DIM), vec),
            pl.BlockSpec((1, QK_HEAD_DIM), vec),
            pl.BlockSpec((1, QK_HEAD_DIM), vec),
            pl.BlockSpec((1, V_HEAD_DIM), vec),
        ] + cast_specs,
        out_specs=[pl.BlockSpec((ATT_TQ, G * 128), lambda h, i: (i, h))] + cast_specs,
        out_shape=[jax.ShapeDtypeStruct((S, ATT_WIDTH), BF16)]
                  + [jax.ShapeDtypeStruct(v.shape, BF16) for v in views],
        scratch_shapes=[
            pltpu.VMEM((G, 256, 2 * ATT_TQ), BF16),
            pltpu.VMEM((G, V_ROWS, ATT_PADW), F32),
            pltpu.VMEM((G, ATT_TK, ATT_PADW), BF16),
            pltpu.VMEM((ATT_TK, 128), BF16),
            pltpu.VMEM((2, G, ATT_TK, ATT_PADW), F32),
        ],
        compiler_params=_params(("arbitrary", "arbitrary")),
        name="diff_attn",
    )(cparts, qT, k, vT, lq1, lk1, lq2, lk2, subln_g, *views)
    return outs[0], [wb.reshape(w.shape) for wb, w in zip(outs[1:], cast_ws)]


def _mix_kernel(o_ref, u_ref, uprev_ref, gate_ref, x_ref, wg_ref, ps_ref, wa_ref, wp_ref, wo_ref,
                gm_ref, x1_ref, hm_ref, ext_ref, *, bm):
    i = pl.program_id(0)
    u = u_ref[...]
    halo = uprev_ref[...]
    ext_ref[0:POOL_HALO, :] = jnp.where(i == 0, jnp.zeros_like(halo), halo)
    ext_ref[POOL_HALO:POOL_HALO + bm, :] = u
    t = i * bm + lax.broadcasted_iota(jnp.int32, (bm, 1), 0)
    pms = []
    for g, w in enumerate(POOL_WINDOWS):
        c0, c1 = g * POOL_GROUP_WIDTH, (g + 1) * POOL_GROUP_WIDTH
        ug = u[:, c0:c1]
        win = ug
        for d in range(1, w):
            win = win + ext_ref[POOL_HALO - d:POOL_HALO - d + bm, c0:c1]
        count = jnp.minimum(t + 1, w).astype(F32)
        pooled = win / count - ug
        pms.append(_dot(pooled.astype(BF16), wg_ref[g]))
    pm = (jnp.concatenate(pms, axis=-1) * ps_ref[...]).astype(BF16)
    p_branch = _dot(pm, wp_ref[...])
    a_branch = _dot(o_ref[...], wa_ref[...])
    gates = gate_ref[...]
    merged = gates[:, :D_MODEL].astype(F32) * a_branch + gates[:, D_MODEL:].astype(F32) * p_branch
    x1 = x_ref[...] + _dot(merged.astype(BF16), wo_ref[...])
    x1_ref[...] = x1
    hm_ref[...] = _rms(x1, gm_ref[...]).astype(BF16)


def _mix(o, u, gates, x, wg, pool_scale, wa, wp, wo, g_mlp, bm=256):
    S = x.shape[0]
    const2 = lambda i: (0, 0)
    once = pl.Buffered(1)
    halo_blocks = bm // POOL_HALO
    return pl.pallas_call(
        functools.partial(_mix_kernel, bm=bm),
        grid=(S // bm,),
        in_specs=[
            pl.BlockSpec((bm, ATT_WIDTH), lambda i: (i, 0)),
            pl.BlockSpec((bm, POOL_WIDTH), lambda i: (i, 0)),
            pl.BlockSpec((POOL_HALO, POOL_WIDTH), lambda i: (jnp.maximum(i * halo_blocks - 1, 0), 0)),
            pl.BlockSpec((bm, 2 * D_MODEL), lambda i: (i, 0)),
            pl.BlockSpec((bm, D_MODEL), lambda i: (i, 0)),
            pl.BlockSpec((len(POOL_WINDOWS), POOL_GROUP_WIDTH, POOL_GROUP_WIDTH), lambda i: (0, 0, 0),
                         pipeline_mode=once),
            pl.BlockSpec((1, POOL_WIDTH), const2),
            pl.BlockSpec((ATT_WIDTH, D_MODEL), const2, pipeline_mode=once),
            pl.BlockSpec((POOL_WIDTH, D_MODEL), const2, pipeline_mode=once),
            pl.BlockSpec((D_MODEL, D_MODEL), const2, pipeline_mode=once),
            pl.BlockSpec((1, D_MODEL), const2),
        ],
        out_specs=[
            pl.BlockSpec((bm, D_MODEL), lambda i: (i, 0)),
            pl.BlockSpec((bm, D_MODEL), lambda i: (i, 0)),
        ],
        out_shape=[
            jax.ShapeDtypeStruct((S, D_MODEL), F32),
            jax.ShapeDtypeStruct((S, D_MODEL), BF16),
        ],
        scratch_shapes=[pltpu.VMEM((POOL_HALO + bm, POOL_WIDTH), F32)],
        compiler_params=_params(("arbitrary",)),
        name="mix_out",
    )(o, u, u, gates, x, wg, pool_scale, wa, wp, wo, g_mlp)


def _mlp_kernel(hm_ref, x1_ref, wu_ref, wd_ref, gp_ref, x2_ref, hp_ref, acc_ref):
    f = pl.program_id(1)

    @pl.when(f == 0)
    def _():
        acc_ref[...] = x1_ref[...]

    a = jnp.maximum(_dot(hm_ref[...], wu_ref[...]), 0.0)
    acc_ref[...] += _dot((a * a).astype(BF16), wd_ref[...])

    @pl.when(f == pl.num_programs(1) - 1)
    def _():
        x2 = acc_ref[...]
        x2_ref[...] = x2
        hp_ref[...] = _rms(x2, gp_ref[...]).astype(BF16)


def _mlp(hm, x1, wu, wd, g_ple, bm=512, bf=1024):
    S = hm.shape[0]
    return pl.pallas_call(
        _mlp_kernel,
        grid=(S // bm, D_FF // bf),
        in_specs=[
            pl.BlockSpec((bm, D_MODEL), lambda i, f: (i, 0)),
            pl.BlockSpec((bm, D_MODEL), lambda i, f: (i, 0)),
            pl.BlockSpec((D_MODEL, bf), lambda i, f: (0, f)),
            pl.BlockSpec((bf, D_MODEL), lambda i, f: (f, 0)),
            pl.BlockSpec((1, D_MODEL), lambda i, f: (0, 0)),
        ],
        out_specs=[
            pl.BlockSpec((bm, D_MODEL), lambda i, f: (i, 0)),
            pl.BlockSpec((bm, D_MODEL), lambda i, f: (i, 0)),
        ],
        out_shape=[
            jax.ShapeDtypeStruct((S, D_MODEL), F32),
            jax.ShapeDtypeStruct((S, D_MODEL), BF16),
        ],
        scratch_shapes=[pltpu.VMEM((bm, D_MODEL), F32)],
        compiler_params=_params(("arbitrary", "arbitrary")),
        name="mlp",
    )(hm, x1, wu, wd, g_ple)


def _ple_kernel(hp_ref, x2_ref, p_ref, wgate_ref, wple_ref, gf_ref, out_ref):
    gate = jax.nn.sigmoid(_dot(hp_ref[...], wgate_ref[...]))
    emb = _dot(p_ref[...].astype(BF16), wple_ref[...])
    x3 = x2_ref[...] + emb * gate
    out_ref[...] = _rms(x3, gf_ref[...])


def _ple(hp, x2, p, wgate, wple, g_final, bm=512):
    S = hp.shape[0]
    const2 = lambda i: (0, 0)
    return pl.pallas_call(
        _ple_kernel,
        grid=(S // bm,),
        in_specs=[
            pl.BlockSpec((bm, D_MODEL), lambda i: (i, 0)),
            pl.BlockSpec((bm, D_MODEL), lambda i: (i, 0)),
            pl.BlockSpec((bm, PLE_DIM), lambda i: (i, 0)),
            pl.BlockSpec((D_MODEL, D_MODEL), const2),
            pl.BlockSpec((PLE_DIM, D_MODEL), const2),
            pl.BlockSpec((1, D_MODEL), const2),
        ],
        out_specs=pl.BlockSpec((bm, D_MODEL), lambda i: (i, 0)),
        out_shape=jax.ShapeDtypeStruct((S, D_MODEL), F32),
        compiler_params=_params(("arbitrary",)),
        name="ple_final",
    )(hp, x2, p, wgate, wple, g_final)


def kernel(x, p, norm_mix_g, w_in, lambda_q1, lambda_k1, lambda_q2, lambda_k2, subln_g, pool_grp_w, pool_scale, w_attn_br, w_pool_br, w_out, norm_mlp_g, w_mlp_up, w_mlp_down, norm_ple_g, w_ple, w_ple_gate, final_norm_g):
    B, S, D = x.shape
    assert (B, S, D) == (1, SEQ, D_MODEL) and norm_mix_g.shape[0] == 1
    cparts = jnp.asarray(_alibi_parts())

    x2d = x[0]
    h = _prenorm(x2d, norm_mix_g)
    qT, k, vT, u, gates = _inproj(h, w_in[0])
    o, (wg, wa, wp, wo, wu, wd, wgate, wple) = _attention(
        cparts, qT, k, vT, lambda_q1, lambda_k1, lambda_q2, lambda_k2, subln_g,
        [pool_grp_w[0], w_attn_br[0], w_pool_br[0], w_out[0], w_mlp_up[0], w_mlp_down[0],
         w_ple_gate[0], w_ple[0]])
    x1, hm = _mix(o, u, gates, x2d, wg, pool_scale, wa, wp, wo, norm_mlp_g)
    x2, hp = _mlp(hm, x1, wu, wd, norm_ple_g)
    out = _ple(hp, x2, p[0, 0], wgate, wple, final_norm_g.reshape(1, D_MODEL))
    return out[None]
```

```python
import functools
import math

import numpy as np
import jax
import jax.numpy as jnp
from jax import lax
from jax.experimental import pallas as pl
from jax.experimental.pallas import tpu as pltpu

F32 = jnp.float32
BF16 = jnp.bfloat16

D_MODEL = 2048
SEQ = 8192
N_HEADS = 8
V_HEAD_DIM = 128
QK_HEAD_DIM = 64
ATT_WIDTH = N_HEADS * V_HEAD_DIM
POOL_WIDTH = 1024
POOL_WINDOWS = (2, 4, 8, 16)
POOL_GROUP_WIDTH = POOL_WIDTH // len(POOL_WINDOWS)
POOL_HALO = 16
D_FF = 4 * D_MODEL
PLE_DIM = 256
NORM_EPS = 1e-6
LAMBDA_INIT = 0.8 - 0.6 * math.exp(-0.3 * 0)
NEG_BIG = -1e30
LOG2E = math.log2(math.e)
V_ROWS = V_HEAD_DIM + 16
BIAS_TERMS = 3


def _alibi_parts():
    import ml_dtypes
    rest = 2.0 ** (-8.0 * np.arange(1, N_HEADS + 1) / N_HEADS) * LOG2E
    parts = []
    for _ in range(BIAS_TERMS):
        piece = rest.astype(ml_dtypes.bfloat16).astype(np.float64)
        parts.append(piece)
        rest = rest - piece
    return np.stack(parts, axis=1).reshape(-1).astype(np.float32)

ATT_TQ = 256
ATT_TK = 512
ATT_G = 4
ATT_PADW = 2 * ATT_TQ
VMEM_LIMIT = 56 * 1024 * 1024
INPROJ_VMEM_LIMIT = 60 * 1024 * 1024


def _rms(xf, g):
    return xf * lax.rsqrt(jnp.mean(xf * xf, axis=-1, keepdims=True) + NORM_EPS) * g


def _dot(a, b):
    return jnp.dot(a, b, preferred_element_type=F32)


def _params(sem, vmem_limit=VMEM_LIMIT):
    return pltpu.CompilerParams(dimension_semantics=sem, vmem_limit_bytes=vmem_limit)


def _prenorm_kernel(x_ref, g_ref, h_ref):
    h_ref[...] = _rms(x_ref[...], g_ref[...]).astype(BF16)


def _prenorm(x, g, bm=512):
    S = x.shape[0]
    return pl.pallas_call(
        _prenorm_kernel,
        grid=(S // bm,),
        in_specs=[pl.BlockSpec((bm, D_MODEL), lambda i: (i, 0)),
                  pl.BlockSpec((1, D_MODEL), lambda i: (0, 0))],
        out_specs=pl.BlockSpec((bm, D_MODEL), lambda i: (i, 0)),
        out_shape=jax.ShapeDtypeStruct((S, D_MODEL), BF16),
        compiler_params=_params(("arbitrary",)),
        name="prenorm",
    )(x, g)


IN_PHASE_COLS = 1024
IN_CHUNK = 256
N_IN_PHASES = 8
GATE_PHASE0 = 4


def _inproj_kernel(h_ref, w_ref, qT_ref, k_ref, vT_ref, u_ref, gate_ref, wb_ref, *, bm):
    j = pl.program_id(0)
    i = pl.program_id(1)
    half = IN_PHASE_COLS // 2

    @pl.when(jnp.logical_and(i == 0, j != 1))
    def _():
        wb_ref[...] = w_ref[...].astype(BF16)

    @pl.when(jnp.logical_and(i == 0, j == 1))
    def _():
        for hh in range(N_HEADS):
            for m in range(2):
                src = m * half + hh * QK_HEAD_DIM
                dst = hh * 128 + m * QK_HEAD_DIM
                wb_ref[:, dst:dst + QK_HEAD_DIM] = w_ref[:, src:src + QK_HEAD_DIM].astype(BF16)

    def chunks():
        h = h_ref[...]
        for c in range(IN_PHASE_COLS // IN_CHUNK):
            yield c, _dot(h, wb_ref[:, c * IN_CHUNK:(c + 1) * IN_CHUNK])

    @pl.when(j == 0)
    def _():
        heads_per_chunk = IN_CHUNK // QK_HEAD_DIM
        for c, z in chunks():
            zT = (z * (QK_HEAD_DIM ** -0.5 * LOG2E)).T.astype(BF16)
            m = (c * IN_CHUNK) // half
            h0 = ((c * IN_CHUNK) % half) // QK_HEAD_DIM
            for hl in range(heads_per_chunk):
                for r in range(bm // ATT_TQ):
                    qT_ref[h0 + hl, r, m * QK_HEAD_DIM:(m + 1) * QK_HEAD_DIM, :] = (
                        zT[hl * QK_HEAD_DIM:(hl + 1) * QK_HEAD_DIM, r * ATT_TQ:(r + 1) * ATT_TQ])

    @pl.when(j == 1)
    def _():
        for c, z in chunks():
            k_ref[:, c * IN_CHUNK:(c + 1) * IN_CHUNK] = z.astype(BF16)

    @pl.when(j == 2)
    def _():
        heads_per_chunk = IN_CHUNK // V_HEAD_DIM
        ones = jnp.ones((V_ROWS - V_HEAD_DIM, ATT_TK), BF16)
        for c, z in chunks():
            zT = z.T.astype(BF16)
            for hl in range(heads_per_chunk):
                hh = c * heads_per_chunk + hl
                for r in range(bm // ATT_TK):
                    vT_ref[hh, r, 0:V_HEAD_DIM, :] = (
                        zT[hl * V_HEAD_DIM:(hl + 1) * V_HEAD_DIM, r * ATT_TK:(r + 1) * ATT_TK])
                    vT_ref[hh, r, V_HEAD_DIM:V_ROWS, :] = ones

    @pl.when(j == 3)
    def _():
        for c, z in chunks():
            u_ref[:, c * IN_CHUNK:(c + 1) * IN_CHUNK] = z

    @pl.when(j >= GATE_PHASE0)
    def _():
        for c, z in chunks():
            gate_ref[:, c * IN_CHUNK:(c + 1) * IN_CHUNK] = jax.nn.sigmoid(z).astype(BF16)


def _inproj(h, w_in, bm=1024):
    S = h.shape[0]
    n_rows = S // bm
    nq, nk = S // ATT_TQ, S // ATT_TK

    def rows_in_phase(phase):
        def f(j, i):
            return jnp.where(j == phase, i, jnp.where(j < phase, 0, n_rows - 1))
        return f

    rq, rk, rv, ru = (rows_in_phase(ph) for ph in range(4))
    return pl.pallas_call(
        functools.partial(_inproj_kernel, bm=bm),
        grid=(N_IN_PHASES, n_rows),
        in_specs=[
            pl.BlockSpec((bm, D_MODEL), lambda j, i: (i, 0)),
            pl.BlockSpec((D_MODEL, IN_PHASE_COLS), lambda j, i: (0, j)),
        ],
        out_specs=[
            pl.BlockSpec((N_HEADS, bm // ATT_TQ, 128, ATT_TQ), lambda j, i: (0, rq(j, i), 0, 0)),
            pl.BlockSpec((bm, 1024), lambda j, i: (rk(j, i), 0)),
            pl.BlockSpec((N_HEADS, bm // ATT_TK, V_ROWS, ATT_TK), lambda j, i: (0, rv(j, i), 0, 0)),
            pl.BlockSpec((bm, POOL_WIDTH), lambda j, i: (ru(j, i), 0)),
            pl.BlockSpec((bm, IN_PHASE_COLS),
                         lambda j, i: (jnp.where(j >= GATE_PHASE0, i, 0), jnp.maximum(j - GATE_PHASE0, 0))),
        ],
        out_shape=[
            jax.ShapeDtypeStruct((N_HEADS, nq, 128, ATT_TQ), BF16),
            jax.ShapeDtypeStruct((S, 1024), BF16),
            jax.ShapeDtypeStruct((N_HEADS, nk, V_ROWS, ATT_TK), BF16),
            jax.ShapeDtypeStruct((S, POOL_WIDTH), F32),
            jax.ShapeDtypeStruct((S, 2 * D_MODEL), BF16),
        ],
        scratch_shapes=[pltpu.VMEM((D_MODEL, IN_PHASE_COLS), BF16)],
        compiler_params=_params(("arbitrary", "arbitrary"), vmem_limit=INPROJ_VMEM_LIMIT),
        name="inproj",
    )(h, w_in)


def _attn_kernel(cpart_ref, qT_ref, k_ref, vT_ref, lq1_ref, lk1_ref, lq2_ref, lk2_ref, g_ref,
                 *rest, n_cast):
    cast_in, (o_ref, *cast_out) = rest[:n_cast], rest[n_cast:2 * n_cast + 1]
    qs_ref, acc_ref, p_ref, feat_ref, s_ref = rest[2 * n_cast + 1:]
    hg = pl.program_id(0)
    qi = pl.program_id(1)
    tq, tk = ATT_TQ, ATT_TK

    for w_ref, wb_ref in zip(cast_in, cast_out):
        wb_ref[...] = w_ref[...].astype(BF16)

    krow = lax.broadcasted_iota(jnp.int32, (tk, 128), 0)
    klane = lax.broadcasted_iota(jnp.int32, (tk, 128), 1)
    feat = jnp.where(klane < BIAS_TERMS, krow >> 4, jnp.where(klane < 2 * BIAS_TERMS, krow & 15, 0))
    feat_ref[...] = feat.astype(F32).astype(BF16)

    row = lax.broadcasted_iota(jnp.int32, (128, tq), 0)
    brow = lax.broadcasted_iota(jnp.int32, (128, 2 * tq), 0)
    slopes = []
    for g in range(ATT_G):
        qT = qT_ref[g]
        zero = jnp.zeros_like(qT)
        qs_ref[g, 0:128, :tq] = jnp.where(row < QK_HEAD_DIM, qT, zero)
        qs_ref[g, 0:128, tq:] = jnp.where(row >= QK_HEAD_DIM, qT, zero)
        parts = [cpart_ref[(hg * ATT_G + g) * BIAS_TERMS + t] for t in range(BIAS_TERMS)]
        bias_rows = jnp.zeros((128, 2 * tq), F32)
        for t in range(BIAS_TERMS):
            bias_rows = jnp.where(brow == t, 16.0 * parts[t], bias_rows)
            bias_rows = jnp.where(brow == BIAS_TERMS + t, parts[t], bias_rows)
        qs_ref[g, 128:256, :] = bias_rows.astype(BF16)
        slopes.append(sum(parts[1:], parts[0]))
    acc_ref[...] = jnp.zeros_like(acc_ref)
    p_ref[...] = jnp.zeros_like(p_ref)

    def load_k(j):
        start = pl.multiple_of(j * tk, tk)
        return [k_ref[pl.ds(start, tk), g * 128:(g + 1) * 128] for g in range(ATT_G)]

    def load_v(j):
        return [vT_ref[g, j] for g in range(ATT_G)]

    def produce(k_blks, par):
        mblks = []
        for g in range(ATT_G):
            k_aug = jnp.concatenate([k_blks[g], feat_ref[...]], axis=1)
            sT = _dot(k_aug, qs_ref[g])
            s_ref[par, g, :, 0:2 * tq] = sT
            mblks.append(jnp.max(sT, axis=0, keepdims=True))
        return tuple(mblks)

    def pv(v_blks, alphas):
        for g in range(ATT_G):
            acc_ref[g, :, 0:2 * tq] = (alphas[g] * acc_ref[g, :, 0:2 * tq]
                                       + _dot(v_blks[g], p_ref[g, :, 0:2 * tq]))

    def consume(j, sTs, mblks, ms, valid):
        new_ms, new_alphas = [], []
        for g in range(ATT_G):
            off = slopes[g] * (j * tk - qi * tq).astype(F32)
            cand = mblks[g] + off
            if valid is not None:
                cand = jnp.where(valid, cand, NEG_BIG)
            m_new = jnp.maximum(ms[g], cand)
            shift = m_new - off
            if valid is not None:
                shift = jnp.where(valid, shift, -NEG_BIG)
            new_alphas.append(jnp.exp2(ms[g] - m_new))
            new_ms.append(m_new)
            p_ref[g, :, 0:2 * tq] = jnp.exp2(sTs[g] - shift).astype(BF16)
        return tuple(new_ms), tuple(new_alphas)

    def half_step(j, par, ms, alphas, mblks, valid):
        k_blks = load_k(jnp.minimum(j + 1, vT_ref.shape[1] - 1))
        v_blks = load_v(jnp.maximum(j - 1, 0))
        mblks_next = produce(k_blks, 1 - par)
        pv(v_blks, alphas)
        ms, alphas = consume(j, [s_ref[par, g, :, 0:2 * tq] for g in range(ATT_G)], mblks, ms, valid)
        return ms, alphas, mblks_next

    n_full = (qi * tq) // tk

    def body(t, carry):
        carry = half_step(2 * t, 0, *carry, None)
        return half_step(2 * t + 1, 1, *carry, 2 * t + 1 < n_full)

    mblks0 = produce(load_k(0), 0)
    init = (tuple(jnp.full((1, 2 * tq), NEG_BIG, F32) for _ in range(ATT_G)),
            tuple(jnp.ones((1, 2 * tq), F32) for _ in range(ATT_G)),
            mblks0)
    ms, alphas, _ = lax.fori_loop(0, (n_full + 1) // 2, body, init)

    pv(load_v(jnp.maximum(n_full - 1, 0)), alphas)
    kpos = lax.broadcasted_iota(jnp.int32, (tk, 2 * tq), 0)
    qcol = lax.broadcasted_iota(jnp.int32, (tk, 2 * tq), 1)
    qpos = jnp.where(qcol >= tq, qcol - tq, qcol) + (qi * tq - n_full * tk)
    causal = kpos <= qpos
    par_last = n_full % 2
    sTs = [jnp.where(causal, s_ref[par_last, g, :, 0:2 * tq], NEG_BIG) for g in range(ATT_G)]
    mblks = [jnp.max(sT, axis=0, keepdims=True) for sT in sTs]
    ms, alphas = consume(n_full, sTs, mblks, ms, None)
    pv(load_v(n_full), alphas)

    lam = (jnp.exp(jnp.sum(lq1_ref[...] * lk1_ref[...], axis=-1, keepdims=True))
           - jnp.exp(jnp.sum(lq2_ref[...] * lk2_ref[...], axis=-1, keepdims=True))
           + LAMBDA_INIT)
    for g in range(ATT_G):
        acc = acc_ref[g, :, 0:2 * tq]
        inv_l = 1.0 / acc[V_HEAD_DIM:V_HEAD_DIM + 1, :]
        num = acc[:V_HEAD_DIM, :] * inv_l
        oT = num[:, :tq] - lam * num[:, tq:]
        o = _rms(oT.T, g_ref[...]) * (1.0 - LAMBDA_INIT)
        o_ref[:, g * 128:(g + 1) * 128] = o.astype(BF16)


def _cast_view(w, n_chunks):
    cols = w.shape[-1]
    rows = w.size // cols
    while rows % (n_chunks * 16):
        assert cols % 256 == 0, w.shape
        cols //= 2
        rows *= 2
    return w.reshape(rows, cols)


def _attention(cparts, qT, k, vT, lq1, lk1, lq2, lk2, subln_g, cast_ws):
    S = k.shape[0]
    nq, nk = S // ATT_TQ, S // ATT_TK
    G = ATT_G
    n_steps = (N_HEADS // G) * nq
    vec = lambda h, i: (0, 0)
    once = pl.Buffered(1)
    views = [_cast_view(w, n_steps) for w in cast_ws]
    cast_specs = [pl.BlockSpec((v.shape[0] // n_steps, v.shape[1]), lambda h, i: (h * nq + i, 0))
                  for v in views]
    outs = pl.pallas_call(
        functools.partial(_attn_kernel, n_cast=len(views)),
        grid=(N_HEADS // G, nq),
        in_specs=[
            pl.BlockSpec(memory_space=pltpu.SMEM),
            pl.BlockSpec((G, None, 128, ATT_TQ), lambda h, i: (h, i, 0, 0)),
            pl.BlockSpec((S, G * 128), lambda h, i: (0, h), pipeline_mode=once),
            pl.BlockSpec((G, nk, V_ROWS, ATT_TK), lambda h, i: (h, 0, 0, 0), pipeline_mode=once),
            pl.BlockSpec((1, QK_HEAD_DIM), vec),
            pl.BlockSpec((1, QK_HEAD_DIM), vec),
            pl.BlockSpec((1, QK_HEAD_DIM), vec),
            pl.BlockSpec((1, QK_HEAD_DIM), vec),
            pl.BlockSpec((1, V_HEAD_DIM), vec),
        ] + cast_specs,
        out_specs=[pl.BlockSpec((ATT_TQ, G * 128), lambda h, i: (i, h))] + cast_specs,
        out_shape=[jax.ShapeDtypeStruct((S, ATT_WIDTH), BF16)]
                  + [jax.ShapeDtypeStruct(v.shape, BF16) for v in views],
        scratch_shapes=[
            pltpu.VMEM((G, 256, 2 * ATT_TQ), BF16),
            pltpu.VMEM((G, V_ROWS, ATT_PADW), F32),
            pltpu.VMEM((G, ATT_TK, ATT_PADW), BF16),
            pltpu.VMEM((ATT_TK, 128), BF16),
            pltpu.VMEM((2, G, ATT_TK, ATT_PADW), F32),
        ],
        compiler_params=_params(("arbitrary", "arbitrary")),
        name="diff_attn",
    )(cparts, qT, k, vT, lq1, lk1, lq2, lk2, subln_g, *views)
    return outs[0], [wb.reshape(w.shape) for wb, w in zip(outs[1:], cast_ws)]


def _mix_kernel(o_ref, u_ref, uprev_ref, gate_ref, x_ref, wg_ref, ps_ref, wa_ref, wp_ref, wo_ref,
                gm_ref, x1_ref, hm_ref, ext_ref, *, bm):
    i = pl.program_id(0)
    u = u_ref[...]
    halo = uprev_ref[...]
    ext_ref[0:POOL_HALO, :] = jnp.where(i == 0, jnp.zeros_like(halo), halo)
    ext_ref[POOL_HALO:POOL_HALO + bm, :] = u
    t = i * bm + lax.broadcasted_iota(jnp.int32, (bm, 1), 0)
    pms = []
    a_chunks = []
    n_groups = len(POOL_WINDOWS)
    a_cols = D_MODEL // n_groups
    o = o_ref[...]
    for g, w in enumerate(POOL_WINDOWS):
        a_chunks.append(_dot(o, wa_ref[:, g * a_cols:(g + 1) * a_cols]))
        c0, c1 = g * POOL_GROUP_WIDTH, (g + 1) * POOL_GROUP_WIDTH
        ug = u[:, c0:c1]
        win = ug
        for d in range(1, w):
            win = win + ext_ref[POOL_HALO - d:POOL_HALO - d + bm, c0:c1]
        count = jnp.minimum(t + 1, w).astype(F32)
        pooled = win / count - ug
        pms.append(_dot(pooled.astype(BF16), wg_ref[g]))
    pm = (jnp.concatenate(pms, axis=-1) * ps_ref[...]).astype(BF16)
    p_branch = _dot(pm, wp_ref[...])
    a_branch = jnp.concatenate(a_chunks, axis=-1)
    gates = gate_ref[...]
    merged = gates[:, :D_MODEL].astype(F32) * a_branch + gates[:, D_MODEL:].astype(F32) * p_branch
    x1 = x_ref[...] + _dot(merged.astype(BF16), wo_ref[...])
    x1_ref[...] = x1
    hm_ref[...] = _rms(x1, gm_ref[...]).astype(BF16)


def _mix(o, u, gates, x, wg, pool_scale, wa, wp, wo, g_mlp, bm=256):
    S = x.shape[0]
    const2 = lambda i: (0, 0)
    once = pl.Buffered(1)
    halo_blocks = bm // POOL_HALO
    return pl.pallas_call(
        functools.partial(_mix_kernel, bm=bm),
        grid=(S // bm,),
        in_specs=[
            pl.BlockSpec((bm, ATT_WIDTH), lambda i: (i, 0)),
            pl.BlockSpec((bm, POOL_WIDTH), lambda i: (i, 0)),
            pl.BlockSpec((POOL_HALO, POOL_WIDTH), lambda i: (jnp.maximum(i * halo_blocks - 1, 0), 0)),
            pl.BlockSpec((bm, 2 * D_MODEL), lambda i: (i, 0)),
            pl.BlockSpec((bm, D_MODEL), lambda i: (i, 0)),
            pl.BlockSpec((len(POOL_WINDOWS), POOL_GROUP_WIDTH, POOL_GROUP_WIDTH), lambda i: (0, 0, 0),
                         pipeline_mode=once),
            pl.BlockSpec((1, POOL_WIDTH), const2),
            pl.BlockSpec((ATT_WIDTH, D_MODEL), const2, pipeline_mode=once),
            pl.BlockSpec((POOL_WIDTH, D_MODEL), const2, pipeline_mode=once),
            pl.BlockSpec((D_MODEL, D_MODEL), const2, pipeline_mode=once),
            pl.BlockSpec((1, D_MODEL), const2),
        ],
        out_specs=[
            pl.BlockSpec((bm, D_MODEL), lambda i: (i, 0)),
            pl.BlockSpec((bm, D_MODEL), lambda i: (i, 0)),
        ],
        out_shape=[
            jax.ShapeDtypeStruct((S, D_MODEL), F32),
            jax.ShapeDtypeStruct((S, D_MODEL), BF16),
        ],
        scratch_shapes=[pltpu.VMEM((POOL_HALO + bm, POOL_WIDTH), F32)],
        compiler_params=_params(("arbitrary",)),
        name="mix_out",
    )(o, u, u, gates, x, wg, pool_scale, wa, wp, wo, g_mlp)


def _mlp_kernel(hm_ref, wu_ref, wd_ref, d_ref):
    @pl.when(pl.program_id(1) == 0)
    def _():
        d_ref[...] = jnp.zeros_like(d_ref)

    a = jnp.maximum(_dot(hm_ref[...], wu_ref[...]), 0.0)
    d_ref[...] += _dot((a * a).astype(BF16), wd_ref[...])


def _mlp(hm, wu, wd, bm=1024, bf=1024):
    S = hm.shape[0]
    return pl.pallas_call(
        _mlp_kernel,
        grid=(S // bm, D_FF // bf),
        in_specs=[
            pl.BlockSpec((bm, D_MODEL), lambda i, f: (i, 0)),
            pl.BlockSpec((D_MODEL, bf), lambda i, f: (0, f)),
            pl.BlockSpec((bf, D_MODEL), lambda i, f: (f, 0)),
        ],
        out_specs=pl.BlockSpec((bm, D_MODEL), lambda i, f: (i, 0)),
        out_shape=jax.ShapeDtypeStruct((S, D_MODEL), F32),
        compiler_params=_params(("arbitrary", "arbitrary")),
        name="mlp",
    )(hm, wu, wd)


PLE_CHUNK = 512


def _ple_kernel(x1_ref, d_ref, p_ref, gp_ref, wgate_ref, wple_ref, gf_ref, out_ref):
    x2 = x1_ref[...] + d_ref[...]
    hp = _rms(x2, gp_ref[...]).astype(BF16)
    pb = p_ref[...].astype(BF16)
    x3 = []
    for c in range(D_MODEL // PLE_CHUNK):
        cols = slice(c * PLE_CHUNK, (c + 1) * PLE_CHUNK)
        gate = jax.nn.sigmoid(_dot(hp, wgate_ref[:, cols]))
        x3.append(x2[:, cols] + _dot(pb, wple_ref[:, cols]) * gate)
    out_ref[...] = _rms(jnp.concatenate(x3, axis=-1), gf_ref[...])


def _ple(x1, d, p, g_ple, wgate, wple, g_final, bm=512):
    S = x1.shape[0]
    const2 = lambda i: (0, 0)
    once = pl.Buffered(1)
    return pl.pallas_call(
        _ple_kernel,
        grid=(S // bm,),
        in_specs=[
            pl.BlockSpec((bm, D_MODEL), lambda i: (i, 0)),
            pl.BlockSpec((bm, D_MODEL), lambda i: (i, 0)),
            pl.BlockSpec((None, None, bm, PLE_DIM), lambda i: (0, 0, i, 0)),
            pl.BlockSpec((1, D_MODEL), const2),
            pl.BlockSpec((D_MODEL, D_MODEL), const2, pipeline_mode=once),
            pl.BlockSpec((PLE_DIM, D_MODEL), const2, pipeline_mode=once),
            pl.BlockSpec((1, D_MODEL), const2),
        ],
        out_specs=pl.BlockSpec((bm, D_MODEL), lambda i: (i, 0)),
        out_shape=jax.ShapeDtypeStruct((S, D_MODEL), F32),
        compiler_params=_params(("arbitrary",)),
        name="ple_final",
    )(x1, d, p, g_ple, wgate, wple, g_final)


def kernel(x, p, norm_mix_g, w_in, lambda_q1, lambda_k1, lambda_q2, lambda_k2, subln_g, pool_grp_w, pool_scale, w_attn_br, w_pool_br, w_out, norm_mlp_g, w_mlp_up, w_mlp_down, norm_ple_g, w_ple, w_ple_gate, final_norm_g):
    B, S, D = x.shape
    assert (B, S, D) == (1, SEQ, D_MODEL) and norm_mix_g.shape[0] == 1
    cparts = jnp.asarray(_alibi_parts())

    x2d = x[0]
    h = _prenorm(x2d, norm_mix_g)
    qT, k, vT, u, gates = _inproj(h, w_in[0])
    o, (wg, wa, wp, wo, wu, wd, wgate, wple) = _attention(
        cparts, qT, k, vT, lambda_q1, lambda_k1, lambda_q2, lambda_k2, subln_g,
        [pool_grp_w[0], w_attn_br[0], w_pool_br[0], w_out[0], w_mlp_up[0], w_mlp_down[0],
         w_ple_gate[0], w_ple[0]])
    x1, hm = _mix(o, u, gates, x2d, wg, pool_scale, wa, wp, wo, norm_mlp_g)
    d = _mlp(hm, wu, wd)
    out = _ple(x1, d, p, norm_ple_g, wgate, wple, final_norm_g.reshape(1, D_MODEL))
    return out[None]
```

```python
import functools
import math

import numpy as np
import jax
import jax.numpy as jnp
from jax import lax
from jax.experimental import pallas as pl
from jax.experimental.pallas import tpu as pltpu

F32 = jnp.float32
BF16 = jnp.bfloat16

D_MODEL = 2048
SEQ = 8192
N_HEADS = 8
V_HEAD_DIM = 128
QK_HEAD_DIM = 64
ATT_WIDTH = N_HEADS * V_HEAD_DIM
POOL_WIDTH = 1024
POOL_WINDOWS = (2, 4, 8, 16)
POOL_GROUP_WIDTH = POOL_WIDTH // len(POOL_WINDOWS)
POOL_HALO = 16
D_FF = 4 * D_MODEL
PLE_DIM = 256
NORM_EPS = 1e-6
LAMBDA_INIT = 0.8 - 0.6 * math.exp(-0.3 * 0)
NEG_BIG = -1e30
LOG2E = math.log2(math.e)
V_ROWS = V_HEAD_DIM + 16
BIAS_TERMS = 3


def _alibi_parts():
    import ml_dtypes
    rest = 2.0 ** (-8.0 * np.arange(1, N_HEADS + 1) / N_HEADS) * LOG2E
    parts = []
    for _ in range(BIAS_TERMS):
        piece = rest.astype(ml_dtypes.bfloat16).astype(np.float64)
        parts.append(piece)
        rest = rest - piece
    return np.stack(parts, axis=1).reshape(-1).astype(np.float32)

ATT_TQ = 256
ATT_TK = 512
ATT_G = 4
ATT_PADW = 2 * ATT_TQ
VMEM_LIMIT = 56 * 1024 * 1024
INPROJ_VMEM_LIMIT = 60 * 1024 * 1024


def _rms(xf, g):
    return xf * lax.rsqrt(jnp.mean(xf * xf, axis=-1, keepdims=True) + NORM_EPS) * g


def _dot(a, b):
    return jnp.dot(a, b, preferred_element_type=F32)


def _params(sem, vmem_limit=VMEM_LIMIT):
    return pltpu.CompilerParams(dimension_semantics=sem, vmem_limit_bytes=vmem_limit)


def _prenorm_kernel(x_ref, g_ref, h_ref):
    h_ref[...] = _rms(x_ref[...], g_ref[...]).astype(BF16)


def _prenorm(x, g, bm=512):
    S = x.shape[0]
    return pl.pallas_call(
        _prenorm_kernel,
        grid=(S // bm,),
        in_specs=[pl.BlockSpec((bm, D_MODEL), lambda i: (i, 0)),
                  pl.BlockSpec((1, D_MODEL), lambda i: (0, 0))],
        out_specs=pl.BlockSpec((bm, D_MODEL), lambda i: (i, 0)),
        out_shape=jax.ShapeDtypeStruct((S, D_MODEL), BF16),
        compiler_params=_params(("arbitrary",)),
        name="prenorm",
    )(x, g)


IN_PHASE_COLS = 1024
IN_CHUNK = 256
N_IN_PHASES = 8
GATE_PHASE0 = 4


def _inproj_kernel(h_ref, w_ref, qT_ref, k_ref, vT_ref, u_ref, gate_ref, wb_ref, *, bm):
    j = pl.program_id(0)
    i = pl.program_id(1)
    half = IN_PHASE_COLS // 2

    @pl.when(jnp.logical_and(i == 0, j != 1))
    def _():
        wb_ref[...] = w_ref[...].astype(BF16)

    @pl.when(jnp.logical_and(i == 0, j == 1))
    def _():
        for hh in range(N_HEADS):
            for m in range(2):
                src = m * half + hh * QK_HEAD_DIM
                dst = hh * 128 + m * QK_HEAD_DIM
                wb_ref[:, dst:dst + QK_HEAD_DIM] = w_ref[:, src:src + QK_HEAD_DIM].astype(BF16)

    def chunks():
        h = h_ref[...]
        for c in range(IN_PHASE_COLS // IN_CHUNK):
            yield c, _dot(h, wb_ref[:, c * IN_CHUNK:(c + 1) * IN_CHUNK])

    @pl.when(j == 0)
    def _():
        heads_per_chunk = IN_CHUNK // QK_HEAD_DIM
        for c, z in chunks():
            zT = (z * (QK_HEAD_DIM ** -0.5 * LOG2E)).T.astype(BF16)
            m = (c * IN_CHUNK) // half
            h0 = ((c * IN_CHUNK) % half) // QK_HEAD_DIM
            for hl in range(heads_per_chunk):
                for r in range(bm // ATT_TQ):
                    qT_ref[h0 + hl, r, m * QK_HEAD_DIM:(m + 1) * QK_HEAD_DIM, :] = (
                        zT[hl * QK_HEAD_DIM:(hl + 1) * QK_HEAD_DIM, r * ATT_TQ:(r + 1) * ATT_TQ])

    @pl.when(j == 1)
    def _():
        for c, z in chunks():
            k_ref[:, c * IN_CHUNK:(c + 1) * IN_CHUNK] = z.astype(BF16)

    @pl.when(j == 2)
    def _():
        heads_per_chunk = IN_CHUNK // V_HEAD_DIM
        ones = jnp.ones((V_ROWS - V_HEAD_DIM, ATT_TK), BF16)
        for c, z in chunks():
            zT = z.T.astype(BF16)
            for hl in range(heads_per_chunk):
                hh = c * heads_per_chunk + hl
                for r in range(bm // ATT_TK):
                    vT_ref[hh, r, 0:V_HEAD_DIM, :] = (
                        zT[hl * V_HEAD_DIM:(hl + 1) * V_HEAD_DIM, r * ATT_TK:(r + 1) * ATT_TK])
                    vT_ref[hh, r, V_HEAD_DIM:V_ROWS, :] = ones

    @pl.when(j == 3)
    def _():
        for c, z in chunks():
            u_ref[:, c * IN_CHUNK:(c + 1) * IN_CHUNK] = z

    @pl.when(j >= GATE_PHASE0)
    def _():
        for c, z in chunks():
            gate_ref[:, c * IN_CHUNK:(c + 1) * IN_CHUNK] = jax.nn.sigmoid(z).astype(BF16)


def _inproj(h, w_in, bm=1024):
    S = h.shape[0]
    n_rows = S // bm
    nq, nk = S // ATT_TQ, S // ATT_TK

    def rows_in_phase(phase):
        def f(j, i):
            return jnp.where(j == phase, i, jnp.where(j < phase, 0, n_rows - 1))
        return f

    rq, rk, rv, ru = (rows_in_phase(ph) for ph in range(4))
    return pl.pallas_call(
        functools.partial(_inproj_kernel, bm=bm),
        grid=(N_IN_PHASES, n_rows),
        in_specs=[
            pl.BlockSpec((bm, D_MODEL), lambda j, i: (i, 0)),
            pl.BlockSpec((D_MODEL, IN_PHASE_COLS), lambda j, i: (0, j)),
        ],
        out_specs=[
            pl.BlockSpec((N_HEADS, bm // ATT_TQ, 128, ATT_TQ), lambda j, i: (0, rq(j, i), 0, 0)),
            pl.BlockSpec((bm, 1024), lambda j, i: (rk(j, i), 0)),
            pl.BlockSpec((N_HEADS, bm // ATT_TK, V_ROWS, ATT_TK), lambda j, i: (0, rv(j, i), 0, 0)),
            pl.BlockSpec((bm, POOL_WIDTH), lambda j, i: (ru(j, i), 0)),
            pl.BlockSpec((bm, IN_PHASE_COLS),
                         lambda j, i: (jnp.where(j >= GATE_PHASE0, i, 0), jnp.maximum(j - GATE_PHASE0, 0))),
        ],
        out_shape=[
            jax.ShapeDtypeStruct((N_HEADS, nq, 128, ATT_TQ), BF16),
            jax.ShapeDtypeStruct((S, 1024), BF16),
            jax.ShapeDtypeStruct((N_HEADS, nk, V_ROWS, ATT_TK), BF16),
            jax.ShapeDtypeStruct((S, POOL_WIDTH), F32),
            jax.ShapeDtypeStruct((S, 2 * D_MODEL), BF16),
        ],
        scratch_shapes=[pltpu.VMEM((D_MODEL, IN_PHASE_COLS), BF16)],
        compiler_params=_params(("arbitrary", "arbitrary"), vmem_limit=INPROJ_VMEM_LIMIT),
        name="inproj",
    )(h, w_in)


def _attn_kernel(cpart_ref, qT_ref, qTn_ref, k_ref, vT_ref, lq1_ref, lk1_ref, lq2_ref, lk2_ref,
                 g_ref, *rest, n_cast):
    cast_in, (o_ref, *cast_out) = rest[:n_cast], rest[n_cast:2 * n_cast + 1]
    qs_ref, acc_ref, p_ref, feat_ref, s_ref, mb_ref = rest[2 * n_cast + 1:]
    hg = pl.program_id(0)
    qi = pl.program_id(1)
    tq, tk = ATT_TQ, ATT_TK

    for w_ref, wb_ref in zip(cast_in, cast_out):
        wb_ref[...] = w_ref[...].astype(BF16)

    parts = [[cpart_ref[(hg * ATT_G + g) * BIAS_TERMS + t] for t in range(BIAS_TERMS)]
             for g in range(ATT_G)]
    slopes = [sum(pt[1:], pt[0]) for pt in parts]

    def load_queries(src_ref):
        row = lax.broadcasted_iota(jnp.int32, (128, tq), 0)
        for g in range(ATT_G):
            qT = src_ref[g]
            zero = jnp.zeros_like(qT)
            qs_ref[g, 0:128, :tq] = jnp.where(row < QK_HEAD_DIM, qT, zero)
            qs_ref[g, 0:128, tq:] = jnp.where(row >= QK_HEAD_DIM, qT, zero)

    acc_ref[...] = jnp.zeros_like(acc_ref)
    p_ref[...] = jnp.zeros_like(p_ref)

    def load_k(j):
        start = pl.multiple_of(j * tk, tk)
        return [k_ref[pl.ds(start, tk), g * 128:(g + 1) * 128] for g in range(ATT_G)]

    def load_v(j):
        return [vT_ref[g, j] for g in range(ATT_G)]

    def produce_head(g, k_blk, par):
        k_aug = jnp.concatenate([k_blk, feat_ref[...]], axis=1)
        sT = _dot(k_aug, qs_ref[g])
        s_ref[par, g, :, 0:2 * tq] = sT
        return jnp.max(sT, axis=0, keepdims=True)

    def produce(k_blks, par):
        return tuple(produce_head(g, k_blks[g], par) for g in range(ATT_G))

    def pv_head(g, v_blk, alpha):
        acc_ref[g, :, 0:2 * tq] = (alpha * acc_ref[g, :, 0:2 * tq]
                                   + _dot(v_blk, p_ref[g, :, 0:2 * tq]))

    def pv(v_blks, alphas):
        for g in range(ATT_G):
            pv_head(g, v_blks[g], alphas[g])

    def consume_head(g, j, sT, mblk, m):
        off = slopes[g] * (j * tk - qi * tq).astype(F32)
        m_new = jnp.maximum(m, mblk + off)
        p_ref[g, :, 0:2 * tq] = jnp.exp2(sT - (m_new - off)).astype(BF16)
        return m_new, jnp.exp2(m - m_new)

    def consume(j, sTs, mblks, ms):
        outs = [consume_head(g, j, sTs[g], mblks[g], ms[g]) for g in range(ATT_G)]
        return tuple(o[0] for o in outs), tuple(o[1] for o in outs)

    def half_step(j, par, ms, alphas, mblks):
        k_blks = load_k(j + 1)
        v_blks = load_v(jnp.maximum(j - 1, 0))
        mblks_next = produce(k_blks, 1 - par)
        pv(v_blks, alphas)
        ms, alphas = consume(j, [s_ref[par, g, :, 0:2 * tq] for g in range(ATT_G)], mblks, ms)
        return ms, alphas, mblks_next

    n_full = (qi * tq) // tk

    def body(t, carry):
        carry = half_step(2 * t, 0, *carry)
        return half_step(2 * t + 1, 1, *carry)

    def produce_block0():
        for g, mblk in enumerate(produce(load_k(0), 0)):
            mb_ref[g] = mblk

    @pl.when(qi == 0)
    def _():
        krow = lax.broadcasted_iota(jnp.int32, (tk, 128), 0)
        klane = lax.broadcasted_iota(jnp.int32, (tk, 128), 1)
        feat = jnp.where(klane < BIAS_TERMS, krow >> 4,
                         jnp.where(klane < 2 * BIAS_TERMS, krow & 15, 0))
        feat_ref[...] = feat.astype(F32).astype(BF16)
        brow = lax.broadcasted_iota(jnp.int32, (128, 2 * tq), 0)
        for g in range(ATT_G):
            bias_rows = jnp.zeros((128, 2 * tq), F32)
            for t in range(BIAS_TERMS):
                bias_rows = jnp.where(brow == t, 16.0 * parts[g][t], bias_rows)
                bias_rows = jnp.where(brow == BIAS_TERMS + t, parts[g][t], bias_rows)
            qs_ref[g, 128:256, :] = bias_rows.astype(BF16)
        load_queries(qT_ref)
        produce_block0()

    init = (tuple(jnp.full((1, 2 * tq), NEG_BIG, F32) for _ in range(ATT_G)),
            tuple(jnp.ones((1, 2 * tq), F32) for _ in range(ATT_G)),
            tuple(mb_ref[g] for g in range(ATT_G)))
    carry = lax.fori_loop(0, n_full // 2, body, init)
    ms, alphas, _ = lax.cond(n_full % 2 == 1,
                             lambda c: half_step(n_full - 1, 0, *c), lambda c: c, carry)

    pv(load_v(jnp.maximum(n_full - 1, 0)), alphas)
    kpos = lax.broadcasted_iota(jnp.int32, (tk, 2 * tq), 0)
    qcol = lax.broadcasted_iota(jnp.int32, (tk, 2 * tq), 1)
    qpos = jnp.where(qcol >= tq, qcol - tq, qcol) + (qi * tq - n_full * tk)
    causal = kpos <= qpos
    par_last = n_full % 2
    load_queries(qTn_ref)
    k0_blks = load_k(0)
    v_blks = load_v(n_full)
    for g in range(ATT_G):
        sT = jnp.where(causal, s_ref[par_last, g, :, 0:2 * tq], NEG_BIG)
        _, alpha = consume_head(g, n_full, sT, jnp.max(sT, axis=0, keepdims=True), ms[g])
        mb_ref[g] = produce_head(g, k0_blks[g], 0)
        pv_head(g, v_blks[g], alpha)

    lam = (jnp.exp(jnp.sum(lq1_ref[...] * lk1_ref[...], axis=-1, keepdims=True))
           - jnp.exp(jnp.sum(lq2_ref[...] * lk2_ref[...], axis=-1, keepdims=True))
           + LAMBDA_INIT)
    for g in range(ATT_G):
        acc = acc_ref[g, :, 0:2 * tq]
        inv_l = 1.0 / acc[V_HEAD_DIM:V_HEAD_DIM + 1, :]
        num = acc[:V_HEAD_DIM, :] * inv_l
        oT = num[:, :tq] - lam * num[:, tq:]
        o = _rms(oT.T, g_ref[...]) * (1.0 - LAMBDA_INIT)
        o_ref[:, g * 128:(g + 1) * 128] = o.astype(BF16)


def _cast_view(w, n_chunks):
    cols = w.shape[-1]
    rows = w.size // cols
    while rows % (n_chunks * 16):
        assert cols % 256 == 0, w.shape
        cols //= 2
        rows *= 2
    return w.reshape(rows, cols)


def _attention(cparts, qT, k, vT, lq1, lk1, lq2, lk2, subln_g, cast_ws):
    S = k.shape[0]
    nq, nk = S // ATT_TQ, S // ATT_TK
    G = ATT_G
    n_steps = (N_HEADS // G) * nq
    vec = lambda h, i: (0, 0)
    once = pl.Buffered(1)
    views = [_cast_view(w, n_steps) for w in cast_ws]
    cast_specs = [pl.BlockSpec((v.shape[0] // n_steps, v.shape[1]), lambda h, i: (h * nq + i, 0))
                  for v in views]
    outs = pl.pallas_call(
        functools.partial(_attn_kernel, n_cast=len(views)),
        grid=(N_HEADS // G, nq),
        in_specs=[
            pl.BlockSpec(memory_space=pltpu.SMEM),
            pl.BlockSpec((G, None, 128, ATT_TQ), lambda h, i: (h, i, 0, 0)),
            pl.BlockSpec((G, None, 128, ATT_TQ), lambda h, i: (h, jnp.minimum(i + 1, nq - 1), 0, 0)),
            pl.BlockSpec((S, G * 128), lambda h, i: (0, h), pipeline_mode=once),
            pl.BlockSpec((G, nk, V_ROWS, ATT_TK), lambda h, i: (h, 0, 0, 0), pipeline_mode=once),
            pl.BlockSpec((1, QK_HEAD_DIM), vec),
            pl.BlockSpec((1, QK_HEAD_DIM), vec),
            pl.BlockSpec((1, QK_HEAD_DIM), vec),
            pl.BlockSpec((1, QK_HEAD_DIM), vec),
            pl.BlockSpec((1, V_HEAD_DIM), vec),
        ] + cast_specs,
        out_specs=[pl.BlockSpec((ATT_TQ, G * 128), lambda h, i: (i, h))] + cast_specs,
        out_shape=[jax.ShapeDtypeStruct((S, ATT_WIDTH), BF16)]
                  + [jax.ShapeDtypeStruct(v.shape, BF16) for v in views],
        scratch_shapes=[
            pltpu.VMEM((G, 256, 2 * ATT_TQ), BF16),
            pltpu.VMEM((G, V_ROWS, ATT_PADW), F32),
            pltpu.VMEM((G, ATT_TK, ATT_PADW), BF16),
            pltpu.VMEM((ATT_TK, 128), BF16),
            pltpu.VMEM((2, G, ATT_TK, ATT_PADW), F32),
            pltpu.VMEM((G, 1, 2 * ATT_TQ), F32),
        ],
        compiler_params=_params(("arbitrary", "arbitrary")),
        name="diff_attn",
    )(cparts, qT, qT, k, vT, lq1, lk1, lq2, lk2, subln_g, *views)
    return outs[0], [wb.reshape(w.shape) for wb, w in zip(outs[1:], cast_ws)]


def _mix_kernel(o_ref, u_ref, uprev_ref, gate_ref, x_ref, wg_ref, ps_ref, wa_ref, wp_ref, wo_ref,
                gm_ref, x1_ref, hm_ref, ext_ref, *, bm):
    i = pl.program_id(0)
    u = u_ref[...]
    halo = uprev_ref[...]
    ext_ref[0:POOL_HALO, :] = jnp.where(i == 0, jnp.zeros_like(halo), halo)
    ext_ref[POOL_HALO:POOL_HALO + bm, :] = u
    t = i * bm + lax.broadcasted_iota(jnp.int32, (bm, 1), 0)
    pms = []
    a_chunks = []
    n_groups = len(POOL_WINDOWS)
    a_cols = D_MODEL // n_groups
    o = o_ref[...]
    for g, w in enumerate(POOL_WINDOWS):
        a_chunks.append(_dot(o, wa_ref[:, g * a_cols:(g + 1) * a_cols]))
        c0, c1 = g * POOL_GROUP_WIDTH, (g + 1) * POOL_GROUP_WIDTH
        ug = u[:, c0:c1]
        win = ug
        for d in range(1, w):
            win = win + ext_ref[POOL_HALO - d:POOL_HALO - d + bm, c0:c1]
        count = jnp.minimum(t + 1, w).astype(F32)
        pooled = win / count - ug
        pms.append(_dot(pooled.astype(BF16), wg_ref[g]))
    pm = (jnp.concatenate(pms, axis=-1) * ps_ref[...]).astype(BF16)
    p_branch = _dot(pm, wp_ref[...])
    a_branch = jnp.concatenate(a_chunks, axis=-1)
    gates = gate_ref[...]
    merged = gates[:, :D_MODEL].astype(F32) * a_branch + gates[:, D_MODEL:].astype(F32) * p_branch
    x1 = x_ref[...] + _dot(merged.astype(BF16), wo_ref[...])
    x1_ref[...] = x1
    hm_ref[...] = _rms(x1, gm_ref[...]).astype(BF16)


def _mix(o, u, gates, x, wg, pool_scale, wa, wp, wo, g_mlp, bm=256):
    S = x.shape[0]
    const2 = lambda i: (0, 0)
    once = pl.Buffered(1)
    halo_blocks = bm // POOL_HALO
    return pl.pallas_call(
        functools.partial(_mix_kernel, bm=bm),
        grid=(S // bm,),
        in_specs=[
            pl.BlockSpec((bm, ATT_WIDTH), lambda i: (i, 0)),
            pl.BlockSpec((bm, POOL_WIDTH), lambda i: (i, 0)),
            pl.BlockSpec((POOL_HALO, POOL_WIDTH), lambda i: (jnp.maximum(i * halo_blocks - 1, 0), 0)),
            pl.BlockSpec((bm, 2 * D_MODEL), lambda i: (i, 0)),
            pl.BlockSpec((bm, D_MODEL), lambda i: (i, 0)),
            pl.BlockSpec((len(POOL_WINDOWS), POOL_GROUP_WIDTH, POOL_GROUP_WIDTH), lambda i: (0, 0, 0),
                         pipeline_mode=once),
            pl.BlockSpec((1, POOL_WIDTH), const2),
            pl.BlockSpec((ATT_WIDTH, D_MODEL), const2, pipeline_mode=once),
            pl.BlockSpec((POOL_WIDTH, D_MODEL), const2, pipeline_mode=once),
            pl.BlockSpec((D_MODEL, D_MODEL), const2, pipeline_mode=once),
            pl.BlockSpec((1, D_MODEL), const2),
        ],
        out_specs=[
            pl.BlockSpec((bm, D_MODEL), lambda i: (i, 0)),
            pl.BlockSpec((bm, D_MODEL), lambda i: (i, 0)),
        ],
        out_shape=[
            jax.ShapeDtypeStruct((S, D_MODEL), F32),
            jax.ShapeDtypeStruct((S, D_MODEL), BF16),
        ],
        scratch_shapes=[pltpu.VMEM((POOL_HALO + bm, POOL_WIDTH), F32)],
        compiler_params=_params(("arbitrary",)),
        name="mix_out",
    )(o, u, u, gates, x, wg, pool_scale, wa, wp, wo, g_mlp)


def _mlp_kernel(hm_ref, wu_ref, wd_ref, d_ref):
    @pl.when(pl.program_id(1) == 0)
    def _():
        d_ref[...] = jnp.zeros_like(d_ref)

    a = jnp.maximum(_dot(hm_ref[...], wu_ref[...]), 0.0)
    d_ref[...] += _dot((a * a).astype(BF16), wd_ref[...])


def _mlp(hm, wu, wd, bm=1024, bf=1024):
    S = hm.shape[0]
    return pl.pallas_call(
        _mlp_kernel,
        grid=(S // bm, D_FF // bf),
        in_specs=[
            pl.BlockSpec((bm, D_MODEL), lambda i, f: (i, 0)),
            pl.BlockSpec((D_MODEL, bf), lambda i, f: (0, f)),
            pl.BlockSpec((bf, D_MODEL), lambda i, f: (f, 0)),
        ],
        out_specs=pl.BlockSpec((bm, D_MODEL), lambda i, f: (i, 0)),
        out_shape=jax.ShapeDtypeStruct((S, D_MODEL), F32),
        compiler_params=_params(("arbitrary", "arbitrary")),
        name="mlp",
    )(hm, wu, wd)


PLE_CHUNK = 512


def _ple_kernel(x1_ref, d_ref, p_ref, gp_ref, wgate_ref, wple_ref, gf_ref, out_ref):
    x2 = x1_ref[...] + d_ref[...]
    hp = _rms(x2, gp_ref[...]).astype(BF16)
    pb = p_ref[...].astype(BF16)
    x3 = []
    for c in range(D_MODEL // PLE_CHUNK):
        cols = slice(c * PLE_CHUNK, (c + 1) * PLE_CHUNK)
        gate = jax.nn.sigmoid(_dot(hp, wgate_ref[:, cols]))
        x3.append(x2[:, cols] + _dot(pb, wple_ref[:, cols]) * gate)
    out_ref[...] = _rms(jnp.concatenate(x3, axis=-1), gf_ref[...])


def _ple(x1, d, p, g_ple, wgate, wple, g_final, bm=512):
    S = x1.shape[0]
    const2 = lambda i: (0, 0)
    once = pl.Buffered(1)
    return pl.pallas_call(
        _ple_kernel,
        grid=(S // bm,),
        in_specs=[
            pl.BlockSpec((bm, D_MODEL), lambda i: (i, 0)),
            pl.BlockSpec((bm, D_MODEL), lambda i: (i, 0)),
            pl.BlockSpec((None, None, bm, PLE_DIM), lambda i: (0, 0, i, 0)),
            pl.BlockSpec((1, D_MODEL), const2),
            pl.BlockSpec((D_MODEL, D_MODEL), const2, pipeline_mode=once),
            pl.BlockSpec((PLE_DIM, D_MODEL), const2, pipeline_mode=once),
            pl.BlockSpec((1, D_MODEL), const2),
        ],
        out_specs=pl.BlockSpec((bm, D_MODEL), lambda i: (i, 0)),
        out_shape=jax.ShapeDtypeStruct((S, D_MODEL), F32),
        compiler_params=_params(("arbitrary",)),
        name="ple_final",
    )(x1, d, p, g_ple, wgate, wple, g_final)


def kernel(x, p, norm_mix_g, w_in, lambda_q1, lambda_k1, lambda_q2, lambda_k2, subln_g, pool_grp_w, pool_scale, w_attn_br, w_pool_br, w_out, norm_mlp_g, w_mlp_up, w_mlp_down, norm_ple_g, w_ple, w_ple_gate, final_norm_g):
    B, S, D = x.shape
    assert (B, S, D) == (1, SEQ, D_MODEL) and norm_mix_g.shape[0] == 1
    cparts = jnp.asarray(_alibi_parts())

    x2d = x[0]
    h = _prenorm(x2d, norm_mix_g)
    qT, k, vT, u, gates = _inproj(h, w_in[0])
    o, (wg, wa, wp, wo, wu, wd, wgate, wple) = _attention(
        cparts, qT, k, vT, lambda_q1, lambda_k1, lambda_q2, lambda_k2, subln_g,
        [pool_grp_w[0], w_attn_br[0], w_pool_br[0], w_out[0], w_mlp_up[0], w_mlp_down[0],
         w_ple_gate[0], w_ple[0]])
    x1, hm = _mix(o, u, gates, x2d, wg, pool_scale, wa, wp, wo, norm_mlp_g)
    d = _mlp(hm, wu, wd)
    out = _ple(x1, d, p, norm_ple_g, wgate, wple, final_norm_g.reshape(1, D_MODEL))
    return out[None]
```

```python
import functools
import math

import numpy as np
import jax
import jax.numpy as jnp
from jax import lax
from jax.experimental import pallas as pl
from jax.experimental.pallas import tpu as pltpu

F32 = jnp.float32
BF16 = jnp.bfloat16

D_MODEL = 2048
SEQ = 8192
N_HEADS = 8
V_HEAD_DIM = 128
QK_HEAD_DIM = 64
ATT_WIDTH = N_HEADS * V_HEAD_DIM
POOL_WIDTH = 1024
POOL_WINDOWS = (2, 4, 8, 16)
POOL_GROUP_WIDTH = POOL_WIDTH // len(POOL_WINDOWS)
POOL_HALO = 16
D_FF = 4 * D_MODEL
PLE_DIM = 256
NORM_EPS = 1e-6
LAMBDA_INIT = 0.8 - 0.6 * math.exp(-0.3 * 0)
NEG_BIG = -1e30
LOG2E = math.log2(math.e)
V_ROWS = V_HEAD_DIM + 16
BIAS_TERMS = 3


def _alibi_parts():
    import ml_dtypes
    rest = 2.0 ** (-8.0 * np.arange(1, N_HEADS + 1) / N_HEADS) * LOG2E
    parts = []
    for _ in range(BIAS_TERMS):
        piece = rest.astype(ml_dtypes.bfloat16).astype(np.float64)
        parts.append(piece)
        rest = rest - piece
    return np.stack(parts, axis=1).reshape(-1).astype(np.float32)

ATT_TQ = 256
ATT_TK = 512
ATT_G = 4
VMEM_LIMIT = 56 * 1024 * 1024
INPROJ_VMEM_LIMIT = 60 * 1024 * 1024


def _rms(xf, g):
    return xf * lax.rsqrt(jnp.mean(xf * xf, axis=-1, keepdims=True) + NORM_EPS) * g


def _dot(a, b):
    return jnp.dot(a, b, preferred_element_type=F32)


def _params(sem, vmem_limit=VMEM_LIMIT):
    return pltpu.CompilerParams(dimension_semantics=sem, vmem_limit_bytes=vmem_limit)


def _prenorm_kernel(x_ref, g_ref, h_ref):
    h_ref[...] = _rms(x_ref[...], g_ref[...]).astype(BF16)


def _prenorm(x, g, bm=512):
    S = x.shape[0]
    return pl.pallas_call(
        _prenorm_kernel,
        grid=(S // bm,),
        in_specs=[pl.BlockSpec((bm, D_MODEL), lambda i: (i, 0)),
                  pl.BlockSpec((1, D_MODEL), lambda i: (0, 0))],
        out_specs=pl.BlockSpec((bm, D_MODEL), lambda i: (i, 0)),
        out_shape=jax.ShapeDtypeStruct((S, D_MODEL), BF16),
        compiler_params=_params(("arbitrary",)),
        name="prenorm",
    )(x, g)


IN_PHASE_COLS = 1024
IN_CHUNK = 256
N_IN_PHASES = 8
GATE_PHASE0 = 4


def _inproj_kernel(h_ref, w_ref, qT_ref, k_ref, vT_ref, u_ref, gate_ref, wb_ref, *, bm):
    j = pl.program_id(0)
    i = pl.program_id(1)
    half = IN_PHASE_COLS // 2

    @pl.when(jnp.logical_and(i == 0, j != 1))
    def _():
        wb_ref[...] = w_ref[...].astype(BF16)

    @pl.when(jnp.logical_and(i == 0, j == 1))
    def _():
        for hh in range(N_HEADS):
            for m in range(2):
                src = m * half + hh * QK_HEAD_DIM
                dst = hh * 128 + m * QK_HEAD_DIM
                wb_ref[:, dst:dst + QK_HEAD_DIM] = w_ref[:, src:src + QK_HEAD_DIM].astype(BF16)

    def chunks():
        h = h_ref[...]
        for c in range(IN_PHASE_COLS // IN_CHUNK):
            yield c, _dot(h, wb_ref[:, c * IN_CHUNK:(c + 1) * IN_CHUNK])

    @pl.when(j == 0)
    def _():
        heads_per_chunk = IN_CHUNK // QK_HEAD_DIM
        for c, z in chunks():
            zT = (z * (QK_HEAD_DIM ** -0.5 * LOG2E)).T.astype(BF16)
            m = (c * IN_CHUNK) // half
            h0 = ((c * IN_CHUNK) % half) // QK_HEAD_DIM
            for hl in range(heads_per_chunk):
                for r in range(bm // ATT_TQ):
                    qT_ref[h0 + hl, r, m * QK_HEAD_DIM:(m + 1) * QK_HEAD_DIM, :] = (
                        zT[hl * QK_HEAD_DIM:(hl + 1) * QK_HEAD_DIM, r * ATT_TQ:(r + 1) * ATT_TQ])

    @pl.when(j == 1)
    def _():
        for c, z in chunks():
            k_ref[:, c * IN_CHUNK:(c + 1) * IN_CHUNK] = z.astype(BF16)

    @pl.when(j == 2)
    def _():
        heads_per_chunk = IN_CHUNK // V_HEAD_DIM
        ones = jnp.ones((V_ROWS - V_HEAD_DIM, ATT_TK), BF16)
        for c, z in chunks():
            zT = z.T.astype(BF16)
            for hl in range(heads_per_chunk):
                hh = c * heads_per_chunk + hl
                for r in range(bm // ATT_TK):
                    vT_ref[hh, r, 0:V_HEAD_DIM, :] = (
                        zT[hl * V_HEAD_DIM:(hl + 1) * V_HEAD_DIM, r * ATT_TK:(r + 1) * ATT_TK])
                    vT_ref[hh, r, V_HEAD_DIM:V_ROWS, :] = ones

    @pl.when(j == 3)
    def _():
        for c, z in chunks():
            u_ref[:, c * IN_CHUNK:(c + 1) * IN_CHUNK] = z

    @pl.when(j >= GATE_PHASE0)
    def _():
        for c, z in chunks():
            gate_ref[:, c * IN_CHUNK:(c + 1) * IN_CHUNK] = jax.nn.sigmoid(z).astype(BF16)


def _inproj(h, w_in, bm=1024):
    S = h.shape[0]
    n_rows = S // bm
    nq, nk = S // ATT_TQ, S // ATT_TK

    def rows_in_phase(phase):
        def f(j, i):
            return jnp.where(j == phase, i, jnp.where(j < phase, 0, n_rows - 1))
        return f

    rq, rk, rv, ru = (rows_in_phase(ph) for ph in range(4))
    return pl.pallas_call(
        functools.partial(_inproj_kernel, bm=bm),
        grid=(N_IN_PHASES, n_rows),
        in_specs=[
            pl.BlockSpec((bm, D_MODEL), lambda j, i: (i, 0)),
            pl.BlockSpec((D_MODEL, IN_PHASE_COLS), lambda j, i: (0, j)),
        ],
        out_specs=[
            pl.BlockSpec((N_HEADS, bm // ATT_TQ, 128, ATT_TQ), lambda j, i: (0, rq(j, i), 0, 0)),
            pl.BlockSpec((bm, 1024), lambda j, i: (rk(j, i), 0)),
            pl.BlockSpec((N_HEADS, bm // ATT_TK, V_ROWS, ATT_TK), lambda j, i: (0, rv(j, i), 0, 0)),
            pl.BlockSpec((bm, POOL_WIDTH), lambda j, i: (ru(j, i), 0)),
            pl.BlockSpec((bm, IN_PHASE_COLS),
                         lambda j, i: (jnp.where(j >= GATE_PHASE0, i, 0), jnp.maximum(j - GATE_PHASE0, 0))),
        ],
        out_shape=[
            jax.ShapeDtypeStruct((N_HEADS, nq, 128, ATT_TQ), BF16),
            jax.ShapeDtypeStruct((S, 1024), BF16),
            jax.ShapeDtypeStruct((N_HEADS, nk, V_ROWS, ATT_TK), BF16),
            jax.ShapeDtypeStruct((S, POOL_WIDTH), F32),
            jax.ShapeDtypeStruct((S, 2 * D_MODEL), BF16),
        ],
        scratch_shapes=[pltpu.VMEM((D_MODEL, IN_PHASE_COLS), BF16)],
        compiler_params=_params(("arbitrary", "arbitrary"), vmem_limit=INPROJ_VMEM_LIMIT),
        name="inproj",
    )(h, w_in)


def _attn_kernel(cpart_ref, qT_ref, qTn_ref, k_ref, vT_ref, lq1_ref, lk1_ref, lq2_ref, lk2_ref,
                 g_ref, *rest, n_cast):
    cast_in, (o_ref, *cast_out) = rest[:n_cast], rest[n_cast:2 * n_cast + 1]
    qs_ref, acc_ref, p_ref, feat_ref, s_ref, mb_ref = rest[2 * n_cast + 1:]
    hg = pl.program_id(0)
    qi = pl.program_id(1)
    tq, tk = ATT_TQ, ATT_TK

    for w_ref, wb_ref in zip(cast_in, cast_out):
        wb_ref[...] = w_ref[...].astype(BF16)

    parts = [[cpart_ref[(hg * ATT_G + g) * BIAS_TERMS + t] for t in range(BIAS_TERMS)]
             for g in range(ATT_G)]
    slopes = [sum(pt[1:], pt[0]) for pt in parts]

    def load_queries(src_ref):
        row = lax.broadcasted_iota(jnp.int32, (128, tq), 0)
        for g in range(ATT_G):
            qT = src_ref[g]
            zero = jnp.zeros_like(qT)
            qs_ref[g, 0:128, :tq] = jnp.where(row < QK_HEAD_DIM, qT, zero)
            qs_ref[g, 0:128, tq:] = jnp.where(row >= QK_HEAD_DIM, qT, zero)

    acc_ref[...] = jnp.zeros_like(acc_ref)
    p_ref[...] = jnp.zeros_like(p_ref)

    def load_k(j):
        start = pl.multiple_of(j * tk, tk)
        return [k_ref[pl.ds(start, tk), g * 128:(g + 1) * 128] for g in range(ATT_G)]

    def load_v(j):
        return [vT_ref[g, j] for g in range(ATT_G)]

    def produce_head(g, k_blk, par):
        k_aug = jnp.concatenate([k_blk, feat_ref[...]], axis=1)
        sT = _dot(k_aug, qs_ref[g])
        s_ref[par, g] = sT
        return jnp.max(sT, axis=0, keepdims=True)

    def produce(k_blks, par):
        return tuple(produce_head(g, k_blks[g], par) for g in range(ATT_G))

    def pv_head(g, v_blk, alpha):
        acc_ref[g] = alpha * acc_ref[g] + _dot(v_blk, p_ref[g])

    def pv(v_blks, alphas):
        for g in range(ATT_G):
            pv_head(g, v_blks[g], alphas[g])

    def consume_head(g, j, sT, mblk, m):
        off = slopes[g] * (j * tk - qi * tq).astype(F32)
        m_new = jnp.maximum(m, mblk + off)
        p_ref[g] = jnp.exp2(sT - (m_new - off)).astype(BF16)
        return m_new, jnp.exp2(m - m_new)

    def consume(j, sTs, mblks, ms):
        outs = [consume_head(g, j, sTs[g], mblks[g], ms[g]) for g in range(ATT_G)]
        return tuple(o[0] for o in outs), tuple(o[1] for o in outs)

    def half_step(j, par, ms, alphas, mblks):
        k_blks = load_k(j + 1)
        v_blks = load_v(jnp.maximum(j - 1, 0))
        mblks_next = produce(k_blks, 1 - par)
        pv(v_blks, alphas)
        ms, alphas = consume(j, [s_ref[par, g] for g in range(ATT_G)], mblks, ms)
        return ms, alphas, mblks_next

    n_full = (qi * tq) // tk

    def pair(j, carry):
        carry = half_step(j, 0, *carry)
        return half_step(j + 1, 1, *carry)

    def body(t, carry):
        return pair(4 * t + 2, pair(4 * t, carry))

    def produce_block0():
        for g, mblk in enumerate(produce(load_k(0), 0)):
            mb_ref[g] = mblk

    @pl.when(qi == 0)
    def _():
        krow = lax.broadcasted_iota(jnp.int32, (tk, 128), 0)
        klane = lax.broadcasted_iota(jnp.int32, (tk, 128), 1)
        feat = jnp.where(klane < BIAS_TERMS, krow >> 4,
                         jnp.where(klane < 2 * BIAS_TERMS, krow & 15, 0))
        feat_ref[...] = feat.astype(F32).astype(BF16)
        brow = lax.broadcasted_iota(jnp.int32, (128, 2 * tq), 0)
        for g in range(ATT_G):
            bias_rows = jnp.zeros((128, 2 * tq), F32)
            for t in range(BIAS_TERMS):
                bias_rows = jnp.where(brow == t, 16.0 * parts[g][t], bias_rows)
                bias_rows = jnp.where(brow == BIAS_TERMS + t, parts[g][t], bias_rows)
            qs_ref[g, 128:256, :] = bias_rows.astype(BF16)
        load_queries(qT_ref)
        produce_block0()

    init = (tuple(jnp.full((1, 2 * tq), NEG_BIG, F32) for _ in range(ATT_G)),
            tuple(jnp.ones((1, 2 * tq), F32) for _ in range(ATT_G)),
            tuple(mb_ref[g] for g in range(ATT_G)))
    carry = lax.fori_loop(0, n_full // 4, body, init)
    carry = lax.cond(n_full % 4 >= 2, lambda c: pair((n_full // 4) * 4, c), lambda c: c, carry)
    ms, alphas, _ = lax.cond(n_full % 2 == 1,
                             lambda c: half_step(n_full - 1, 0, *c), lambda c: c, carry)

    pv(load_v(jnp.maximum(n_full - 1, 0)), alphas)
    kpos = lax.broadcasted_iota(jnp.int32, (tk, 2 * tq), 0)
    qcol = lax.broadcasted_iota(jnp.int32, (tk, 2 * tq), 1)
    qpos = jnp.where(qcol >= tq, qcol - tq, qcol) + (qi * tq - n_full * tk)
    causal = kpos <= qpos
    par_last = n_full % 2
    load_queries(qTn_ref)
    k0_blks = load_k(0)
    v_blks = load_v(n_full)
    for g in range(ATT_G):
        sT = jnp.where(causal, s_ref[par_last, g], NEG_BIG)
        _, alpha = consume_head(g, n_full, sT, jnp.max(sT, axis=0, keepdims=True), ms[g])
        mb_ref[g] = produce_head(g, k0_blks[g], 0)
        pv_head(g, v_blks[g], alpha)

    lam = (jnp.exp(jnp.sum(lq1_ref[...] * lk1_ref[...], axis=-1, keepdims=True))
           - jnp.exp(jnp.sum(lq2_ref[...] * lk2_ref[...], axis=-1, keepdims=True))
           + LAMBDA_INIT)
    for g in range(ATT_G):
        acc = acc_ref[g]
        inv_l = 1.0 / acc[V_HEAD_DIM:V_HEAD_DIM + 1, :]
        num = acc[:V_HEAD_DIM, :] * inv_l
        oT = num[:, :tq] - lam * num[:, tq:]
        o = _rms(oT.T, g_ref[...]) * (1.0 - LAMBDA_INIT)
        o_ref[:, g * 128:(g + 1) * 128] = o.astype(BF16)


def _cast_view(w, n_chunks):
    cols = w.shape[-1]
    rows = w.size // cols
    while rows % (n_chunks * 16):
        assert cols % 256 == 0, w.shape
        cols //= 2
        rows *= 2
    return w.reshape(rows, cols)


def _attention(cparts, qT, k, vT, lq1, lk1, lq2, lk2, subln_g, cast_ws):
    S = k.shape[0]
    nq, nk = S // ATT_TQ, S // ATT_TK
    G = ATT_G
    n_steps = (N_HEADS // G) * nq
    vec = lambda h, i: (0, 0)
    once = pl.Buffered(1)
    views = [_cast_view(w, n_steps) for w in cast_ws]
    cast_specs = [pl.BlockSpec((v.shape[0] // n_steps, v.shape[1]), lambda h, i: (h * nq + i, 0))
                  for v in views]
    outs = pl.pallas_call(
        functools.partial(_attn_kernel, n_cast=len(views)),
        grid=(N_HEADS // G, nq),
        in_specs=[
            pl.BlockSpec(memory_space=pltpu.SMEM),
            pl.BlockSpec((G, None, 128, ATT_TQ), lambda h, i: (h, i, 0, 0)),
            pl.BlockSpec((G, None, 128, ATT_TQ), lambda h, i: (h, jnp.minimum(i + 1, nq - 1), 0, 0)),
            pl.BlockSpec((S, G * 128), lambda h, i: (0, h), pipeline_mode=once),
            pl.BlockSpec((G, nk, V_ROWS, ATT_TK), lambda h, i: (h, 0, 0, 0), pipeline_mode=once),
            pl.BlockSpec((1, QK_HEAD_DIM), vec),
            pl.BlockSpec((1, QK_HEAD_DIM), vec),
            pl.BlockSpec((1, QK_HEAD_DIM), vec),
            pl.BlockSpec((1, QK_HEAD_DIM), vec),
            pl.BlockSpec((1, V_HEAD_DIM), vec),
        ] + cast_specs,
        out_specs=[pl.BlockSpec((ATT_TQ, G * 128), lambda h, i: (i, h))] + cast_specs,
        out_shape=[jax.ShapeDtypeStruct((S, ATT_WIDTH), BF16)]
                  + [jax.ShapeDtypeStruct(v.shape, BF16) for v in views],
        scratch_shapes=[
            pltpu.VMEM((G, 256, 2 * ATT_TQ), BF16),
            pltpu.VMEM((G, V_ROWS, 2 * ATT_TQ), F32),
            pltpu.VMEM((G, ATT_TK, 2 * ATT_TQ), BF16),
            pltpu.VMEM((ATT_TK, 128), BF16),
            pltpu.VMEM((2, G, ATT_TK, 2 * ATT_TQ), F32),
            pltpu.VMEM((G, 1, 2 * ATT_TQ), F32),
        ],
        compiler_params=_params(("arbitrary", "arbitrary")),
        name="diff_attn",
    )(cparts, qT, qT, k, vT, lq1, lk1, lq2, lk2, subln_g, *views)
    return outs[0], [wb.reshape(w.shape) for wb, w in zip(outs[1:], cast_ws)]


def _mix_kernel(o_ref, u_ref, uprev_ref, gate_ref, x_ref, wg_ref, ps_ref, wa_ref, wp_ref, wo_ref,
                gm_ref, x1_ref, hm_ref, ext_ref, *, bm):
    i = pl.program_id(0)
    u = u_ref[...]
    halo = uprev_ref[...]
    ext_ref[0:POOL_HALO, :] = jnp.where(i == 0, jnp.zeros_like(halo), halo)
    ext_ref[POOL_HALO:POOL_HALO + bm, :] = u
    t = i * bm + lax.broadcasted_iota(jnp.int32, (bm, 1), 0)
    pms = []
    a_chunks = []
    n_groups = len(POOL_WINDOWS)
    a_cols = D_MODEL // n_groups
    o = o_ref[...]
    for g, w in enumerate(POOL_WINDOWS):
        a_chunks.append(_dot(o, wa_ref[:, g * a_cols:(g + 1) * a_cols]))
        c0, c1 = g * POOL_GROUP_WIDTH, (g + 1) * POOL_GROUP_WIDTH
        ug = u[:, c0:c1]
        win = ug
        for d in range(1, w):
            win = win + ext_ref[POOL_HALO - d:POOL_HALO - d + bm, c0:c1]
        count = jnp.minimum(t + 1, w).astype(F32)
        pooled = win / count - ug
        pms.append(_dot(pooled.astype(BF16), wg_ref[g]))
    pm = (jnp.concatenate(pms, axis=-1) * ps_ref[...]).astype(BF16)
    p_branch = _dot(pm, wp_ref[...])
    a_branch = jnp.concatenate(a_chunks, axis=-1)
    gates = gate_ref[...]
    merged = gates[:, :D_MODEL].astype(F32) * a_branch + gates[:, D_MODEL:].astype(F32) * p_branch
    x1 = x_ref[...] + _dot(merged.astype(BF16), wo_ref[...])
    x1_ref[...] = x1
    hm_ref[...] = _rms(x1, gm_ref[...]).astype(BF16)


def _mix(o, u, gates, x, wg, pool_scale, wa, wp, wo, g_mlp, bm=256):
    S = x.shape[0]
    const2 = lambda i: (0, 0)
    once = pl.Buffered(1)
    halo_blocks = bm // POOL_HALO
    return pl.pallas_call(
        functools.partial(_mix_kernel, bm=bm),
        grid=(S // bm,),
        in_specs=[
            pl.BlockSpec((bm, ATT_WIDTH), lambda i: (i, 0)),
            pl.BlockSpec((bm, POOL_WIDTH), lambda i: (i, 0)),
            pl.BlockSpec((POOL_HALO, POOL_WIDTH), lambda i: (jnp.maximum(i * halo_blocks - 1, 0), 0)),
            pl.BlockSpec((bm, 2 * D_MODEL), lambda i: (i, 0)),
            pl.BlockSpec((bm, D_MODEL), lambda i: (i, 0)),
            pl.BlockSpec((len(POOL_WINDOWS), POOL_GROUP_WIDTH, POOL_GROUP_WIDTH), lambda i: (0, 0, 0),
                         pipeline_mode=once),
            pl.BlockSpec((1, POOL_WIDTH), const2),
            pl.BlockSpec((ATT_WIDTH, D_MODEL), const2, pipeline_mode=once),
            pl.BlockSpec((POOL_WIDTH, D_MODEL), const2, pipeline_mode=once),
            pl.BlockSpec((D_MODEL, D_MODEL), const2, pipeline_mode=once),
            pl.BlockSpec((1, D_MODEL), const2),
        ],
        out_specs=[
            pl.BlockSpec((bm, D_MODEL), lambda i: (i, 0)),
            pl.BlockSpec((bm, D_MODEL), lambda i: (i, 0)),
        ],
        out_shape=[
            jax.ShapeDtypeStruct((S, D_MODEL), F32),
            jax.ShapeDtypeStruct((S, D_MODEL), BF16),
        ],
        scratch_shapes=[pltpu.VMEM((POOL_HALO + bm, POOL_WIDTH), F32)],
        compiler_params=_params(("arbitrary",)),
        name="mix_out",
    )(o, u, u, gates, x, wg, pool_scale, wa, wp, wo, g_mlp)


def _mlp_kernel(hm_ref, wu_ref, wd_ref, d_ref):
    @pl.when(pl.program_id(1) == 0)
    def _():
        d_ref[...] = jnp.zeros_like(d_ref)

    a = jnp.maximum(_dot(hm_ref[...], wu_ref[...]), 0.0)
    d_ref[...] += _dot((a * a).astype(BF16), wd_ref[...])


def _mlp(hm, wu, wd, bm=1024, bf=1024):
    S = hm.shape[0]
    return pl.pallas_call(
        _mlp_kernel,
        grid=(S // bm, D_FF // bf),
        in_specs=[
            pl.BlockSpec((bm, D_MODEL), lambda i, f: (i, 0)),
            pl.BlockSpec((D_MODEL, bf), lambda i, f: (0, f)),
            pl.BlockSpec((bf, D_MODEL), lambda i, f: (f, 0)),
        ],
        out_specs=pl.BlockSpec((bm, D_MODEL), lambda i, f: (i, 0)),
        out_shape=jax.ShapeDtypeStruct((S, D_MODEL), F32),
        compiler_params=_params(("arbitrary", "arbitrary")),
        name="mlp",
    )(hm, wu, wd)


PLE_CHUNK = 512


def _ple_kernel(x1_ref, d_ref, p_ref, gp_ref, wgate_ref, wple_ref, gf_ref, out_ref, *, bm):
    n_chunks = D_MODEL // PLE_CHUNK
    halves = [slice(0, bm // 2), slice(bm // 2, bm)]

    def prenorm(rows):
        x2 = x1_ref[rows, :] + d_ref[rows, :]
        return x2, _rms(x2, gp_ref[...]).astype(BF16), p_ref[rows, :].astype(BF16)

    def chunk(state, c):
        x2, hp, pb = state
        cols = slice(c * PLE_CHUNK, (c + 1) * PLE_CHUNK)
        gate = jax.nn.sigmoid(_dot(hp, wgate_ref[:, cols]))
        return x2[:, cols] + _dot(pb, wple_ref[:, cols]) * gate

    def finish(rows, x3):
        out_ref[rows, :] = _rms(jnp.concatenate(x3, axis=-1), gf_ref[...])

    a = prenorm(halves[0])
    xa = [chunk(a, 0)]
    b = prenorm(halves[1])
    xa += [chunk(a, c) for c in range(1, n_chunks)]
    xb = [chunk(b, 0)]
    finish(halves[0], xa)
    xb += [chunk(b, c) for c in range(1, n_chunks)]
    finish(halves[1], xb)


def _ple(x1, d, p, g_ple, wgate, wple, g_final, bm=512):
    S = x1.shape[0]
    const2 = lambda i: (0, 0)
    once = pl.Buffered(1)
    return pl.pallas_call(
        functools.partial(_ple_kernel, bm=bm),
        grid=(S // bm,),
        in_specs=[
            pl.BlockSpec((bm, D_MODEL), lambda i: (i, 0)),
            pl.BlockSpec((bm, D_MODEL), lambda i: (i, 0)),
            pl.BlockSpec((None, None, bm, PLE_DIM), lambda i: (0, 0, i, 0)),
            pl.BlockSpec((1, D_MODEL), const2),
            pl.BlockSpec((D_MODEL, D_MODEL), const2, pipeline_mode=once),
            pl.BlockSpec((PLE_DIM, D_MODEL), const2, pipeline_mode=once),
            pl.BlockSpec((1, D_MODEL), const2),
        ],
        out_specs=pl.BlockSpec((bm, D_MODEL), lambda i: (i, 0)),
        out_shape=jax.ShapeDtypeStruct((S, D_MODEL), F32),
        compiler_params=_params(("arbitrary",)),
        name="ple_final",
    )(x1, d, p, g_ple, wgate, wple, g_final)


def kernel(x, p, norm_mix_g, w_in, lambda_q1, lambda_k1, lambda_q2, lambda_k2, subln_g, pool_grp_w, pool_scale, w_attn_br, w_pool_br, w_out, norm_mlp_g, w_mlp_up, w_mlp_down, norm_ple_g, w_ple, w_ple_gate, final_norm_g):
    B, S, D = x.shape
    assert (B, S, D) == (1, SEQ, D_MODEL) and norm_mix_g.shape[0] == 1
    cparts = jnp.asarray(_alibi_parts())

    x2d = x[0]
    h = _prenorm(x2d, norm_mix_g)
    qT, k, vT, u, gates = _inproj(h, w_in[0])
    o, (wg, wa, wp, wo, wu, wd, wgate, wple) = _attention(
        cparts, qT, k, vT, lambda_q1, lambda_k1, lambda_q2, lambda_k2, subln_g,
        [pool_grp_w[0], w_attn_br[0], w_pool_br[0], w_out[0], w_mlp_up[0], w_mlp_down[0],
         w_ple_gate[0], w_ple[0]])
    x1, hm = _mix(o, u, gates, x2d, wg, pool_scale, wa, wp, wo, norm_mlp_g)
    d = _mlp(hm, wu, wd)
    out = _ple(x1, d, p, norm_ple_g, wgate, wple, final_norm_g.reshape(1, D_MODEL))
    return out[None]
```

```python
import functools
import math

import numpy as np
import jax
import jax.numpy as jnp
from jax import lax
from jax.experimental import pallas as pl
from jax.experimental.pallas import tpu as pltpu

F32 = jnp.float32
BF16 = jnp.bfloat16

D_MODEL = 2048
SEQ = 8192
N_HEADS = 8
V_HEAD_DIM = 128
QK_HEAD_DIM = 64
ATT_WIDTH = N_HEADS * V_HEAD_DIM
POOL_WIDTH = 1024
POOL_WINDOWS = (2, 4, 8, 16)
POOL_GROUP_WIDTH = POOL_WIDTH // len(POOL_WINDOWS)
POOL_HALO = 16
D_FF = 4 * D_MODEL
PLE_DIM = 256
NORM_EPS = 1e-6
LAMBDA_INIT = 0.8 - 0.6 * math.exp(-0.3 * 0)
NEG_BIG = -1e30
LOG2E = math.log2(math.e)
V_ROWS = V_HEAD_DIM + 16
BIAS_TERMS = 3


def _alibi_parts():
    import ml_dtypes
    rest = 2.0 ** (-8.0 * np.arange(1, N_HEADS + 1) / N_HEADS) * LOG2E
    parts = []
    for _ in range(BIAS_TERMS):
        piece = rest.astype(ml_dtypes.bfloat16).astype(np.float64)
        parts.append(piece)
        rest = rest - piece
    return np.stack(parts, axis=1).reshape(-1).astype(np.float32)

ATT_TQ = 256
ATT_TK = 512
ATT_G = 4
VMEM_LIMIT = 56 * 1024 * 1024
INPROJ_VMEM_LIMIT = 60 * 1024 * 1024


def _rms(xf, g):
    return xf * lax.rsqrt(jnp.mean(xf * xf, axis=-1, keepdims=True) + NORM_EPS) * g


def _dot(a, b):
    return jnp.dot(a, b, preferred_element_type=F32)


def _sigmoid(z):
    return 0.5 * jnp.tanh(0.5 * z) + 0.5


def _params(sem, vmem_limit=VMEM_LIMIT):
    return pltpu.CompilerParams(dimension_semantics=sem, vmem_limit_bytes=vmem_limit)


def _prenorm_kernel(x_ref, g_ref, h_ref):
    h_ref[...] = _rms(x_ref[...], g_ref[...]).astype(BF16)


def _prenorm(x, g, bm=512):
    S = x.shape[0]
    return pl.pallas_call(
        _prenorm_kernel,
        grid=(S // bm,),
        in_specs=[pl.BlockSpec((bm, D_MODEL), lambda i: (i, 0)),
                  pl.BlockSpec((1, D_MODEL), lambda i: (0, 0))],
        out_specs=pl.BlockSpec((bm, D_MODEL), lambda i: (i, 0)),
        out_shape=jax.ShapeDtypeStruct((S, D_MODEL), BF16),
        compiler_params=_params(("arbitrary",)),
        name="prenorm",
    )(x, g)


IN_PHASE_COLS = 1024
IN_CHUNK = 256
N_IN_PHASES = 8
GATE_PHASE0 = 4


def _inproj_kernel(h_ref, w_ref, qT_ref, k_ref, vT_ref, u_ref, gate_ref, wb_ref, *, bm):
    j = pl.program_id(0)
    i = pl.program_id(1)
    half = IN_PHASE_COLS // 2

    @pl.when(jnp.logical_and(i == 0, j == 1))
    def _():
        for hh in range(N_HEADS):
            for m in range(2):
                src = m * half + hh * QK_HEAD_DIM
                dst = hh * 128 + m * QK_HEAD_DIM
                wb_ref[:, dst:dst + QK_HEAD_DIM] = w_ref[:, src:src + QK_HEAD_DIM].astype(BF16)

    def chunks(regrouped=False):
        h = h_ref[...]
        for c in range(IN_PHASE_COLS // IN_CHUNK):
            cols = slice(c * IN_CHUNK, (c + 1) * IN_CHUNK)
            w = wb_ref[:, cols] if regrouped else w_ref[:, cols].astype(BF16)
            yield c, _dot(h, w)

    @pl.when(j == 0)
    def _():
        heads_per_chunk = IN_CHUNK // QK_HEAD_DIM
        for c, z in chunks():
            zT = (z * (QK_HEAD_DIM ** -0.5 * LOG2E)).T.astype(BF16)
            m = (c * IN_CHUNK) // half
            h0 = ((c * IN_CHUNK) % half) // QK_HEAD_DIM
            for hl in range(heads_per_chunk):
                for r in range(bm // ATT_TQ):
                    qT_ref[h0 + hl, r, m * QK_HEAD_DIM:(m + 1) * QK_HEAD_DIM, :] = (
                        zT[hl * QK_HEAD_DIM:(hl + 1) * QK_HEAD_DIM, r * ATT_TQ:(r + 1) * ATT_TQ])

    @pl.when(j == 1)
    def _():
        for c, z in chunks(regrouped=True):
            k_ref[:, c * IN_CHUNK:(c + 1) * IN_CHUNK] = z.astype(BF16)

    @pl.when(j == 2)
    def _():
        heads_per_chunk = IN_CHUNK // V_HEAD_DIM
        ones = jnp.ones((V_ROWS - V_HEAD_DIM, ATT_TK), BF16)
        for c, z in chunks():
            zT = z.T.astype(BF16)
            for hl in range(heads_per_chunk):
                hh = c * heads_per_chunk + hl
                for r in range(bm // ATT_TK):
                    vT_ref[hh, r, 0:V_HEAD_DIM, :] = (
                        zT[hl * V_HEAD_DIM:(hl + 1) * V_HEAD_DIM, r * ATT_TK:(r + 1) * ATT_TK])
                    vT_ref[hh, r, V_HEAD_DIM:V_ROWS, :] = ones

    @pl.when(j == 3)
    def _():
        for c, z in chunks():
            u_ref[:, c * IN_CHUNK:(c + 1) * IN_CHUNK] = z

    @pl.when(j >= GATE_PHASE0)
    def _():
        for c, z in chunks():
            gate_ref[:, c * IN_CHUNK:(c + 1) * IN_CHUNK] = _sigmoid(z).astype(BF16)


def _inproj(h, w_in, bm=1024):
    S = h.shape[0]
    n_rows = S // bm
    nq, nk = S // ATT_TQ, S // ATT_TK

    def rows_in_phase(phase):
        def f(j, i):
            return jnp.where(j == phase, i, jnp.where(j < phase, 0, n_rows - 1))
        return f

    rq, rk, rv, ru = (rows_in_phase(ph) for ph in range(4))
    return pl.pallas_call(
        functools.partial(_inproj_kernel, bm=bm),
        grid=(N_IN_PHASES, n_rows),
        in_specs=[
            pl.BlockSpec((bm, D_MODEL), lambda j, i: (i, 0)),
            pl.BlockSpec((D_MODEL, IN_PHASE_COLS), lambda j, i: (0, j)),
        ],
        out_specs=[
            pl.BlockSpec((N_HEADS, bm // ATT_TQ, 128, ATT_TQ), lambda j, i: (0, rq(j, i), 0, 0)),
            pl.BlockSpec((bm, 1024), lambda j, i: (rk(j, i), 0)),
            pl.BlockSpec((N_HEADS, bm // ATT_TK, V_ROWS, ATT_TK), lambda j, i: (0, rv(j, i), 0, 0)),
            pl.BlockSpec((bm, POOL_WIDTH), lambda j, i: (ru(j, i), 0)),
            pl.BlockSpec((bm, IN_PHASE_COLS),
                         lambda j, i: (jnp.where(j >= GATE_PHASE0, i, 0), jnp.maximum(j - GATE_PHASE0, 0))),
        ],
        out_shape=[
            jax.ShapeDtypeStruct((N_HEADS, nq, 128, ATT_TQ), BF16),
            jax.ShapeDtypeStruct((S, 1024), BF16),
            jax.ShapeDtypeStruct((N_HEADS, nk, V_ROWS, ATT_TK), BF16),
            jax.ShapeDtypeStruct((S, POOL_WIDTH), F32),
            jax.ShapeDtypeStruct((S, 2 * D_MODEL), BF16),
        ],
        scratch_shapes=[pltpu.VMEM((D_MODEL, IN_PHASE_COLS), BF16)],
        compiler_params=_params(("arbitrary", "arbitrary"), vmem_limit=INPROJ_VMEM_LIMIT),
        name="inproj",
    )(h, w_in)


def _attn_kernel(cpart_ref, qT_ref, qTn_ref, k_ref, vT_ref, lq1_ref, lk1_ref, lq2_ref, lk2_ref,
                 g_ref, *rest, n_cast):
    cast_in, (o_ref, *cast_out) = rest[:n_cast], rest[n_cast:2 * n_cast + 1]
    qs_ref, acc_ref, p_ref, feat_ref, s_ref, mb_ref = rest[2 * n_cast + 1:]
    hg = pl.program_id(0)
    qi = pl.program_id(1)
    tq, tk = ATT_TQ, ATT_TK

    for w_ref, wb_ref in zip(cast_in, cast_out):
        wb_ref[...] = w_ref[...].astype(BF16)

    parts = [[cpart_ref[(hg * ATT_G + g) * BIAS_TERMS + t] for t in range(BIAS_TERMS)]
             for g in range(ATT_G)]
    slopes = [sum(pt[1:], pt[0]) for pt in parts]

    def load_queries(src_ref):
        row = lax.broadcasted_iota(jnp.int32, (128, tq), 0)
        for g in range(ATT_G):
            qT = src_ref[g]
            zero = jnp.zeros_like(qT)
            qs_ref[g, 0:128, :tq] = jnp.where(row < QK_HEAD_DIM, qT, zero)
            qs_ref[g, 0:128, tq:] = jnp.where(row >= QK_HEAD_DIM, qT, zero)

    acc_ref[...] = jnp.zeros_like(acc_ref)
    p_ref[...] = jnp.zeros_like(p_ref)

    def load_k(j):
        start = pl.multiple_of(j * tk, tk)
        return [k_ref[pl.ds(start, tk), g * 128:(g + 1) * 128] for g in range(ATT_G)]

    def load_v(j):
        return [vT_ref[g, j] for g in range(ATT_G)]

    def produce_head(g, k_blk, par):
        k_aug = jnp.concatenate([k_blk, feat_ref[...]], axis=1)
        sT = _dot(k_aug, qs_ref[g])
        s_ref[par, g] = sT
        return jnp.max(sT, axis=0, keepdims=True)

    def produce(k_blks, par):
        return tuple(produce_head(g, k_blks[g], par) for g in range(ATT_G))

    def pv_head(g, v_blk, alpha):
        acc_ref[g] = alpha * acc_ref[g] + _dot(v_blk, p_ref[g])

    def pv(v_blks, alphas):
        for g in range(ATT_G):
            pv_head(g, v_blks[g], alphas[g])

    def consume_head(g, j, sT, mblk, m):
        off = slopes[g] * (j * tk - qi * tq).astype(F32)
        m_new = jnp.maximum(m, mblk + off)
        p_ref[g] = jnp.exp2(sT - (m_new - off)).astype(BF16)
        return m_new, jnp.exp2(m - m_new)

    def consume(j, sTs, mblks, ms):
        outs = [consume_head(g, j, sTs[g], mblks[g], ms[g]) for g in range(ATT_G)]
        return tuple(o[0] for o in outs), tuple(o[1] for o in outs)

    def half_step(j, par, ms, alphas, mblks):
        k_blks = load_k(j + 1)
        v_blks = load_v(jnp.maximum(j - 1, 0))
        mblks_next = produce(k_blks, 1 - par)
        pv(v_blks, alphas)
        ms, alphas = consume(j, [s_ref[par, g] for g in range(ATT_G)], mblks, ms)
        return ms, alphas, mblks_next

    n_full = (qi * tq) // tk

    def pair(j, carry):
        carry = half_step(j, 0, *carry)
        return half_step(j + 1, 1, *carry)

    def body(t, carry):
        return pair(4 * t + 2, pair(4 * t, carry))

    def produce_block0():
        for g, mblk in enumerate(produce(load_k(0), 0)):
            mb_ref[g] = mblk

    @pl.when(qi == 0)
    def _():
        krow = lax.broadcasted_iota(jnp.int32, (tk, 128), 0)
        klane = lax.broadcasted_iota(jnp.int32, (tk, 128), 1)
        feat = jnp.where(klane < BIAS_TERMS, krow >> 4,
                         jnp.where(klane < 2 * BIAS_TERMS, krow & 15, 0))
        feat_ref[...] = feat.astype(F32).astype(BF16)
        brow = lax.broadcasted_iota(jnp.int32, (128, 2 * tq), 0)
        for g in range(ATT_G):
            bias_rows = jnp.zeros((128, 2 * tq), F32)
            for t in range(BIAS_TERMS):
                bias_rows = jnp.where(brow == t, 16.0 * parts[g][t], bias_rows)
                bias_rows = jnp.where(brow == BIAS_TERMS + t, parts[g][t], bias_rows)
            qs_ref[g, 128:256, :] = bias_rows.astype(BF16)
        load_queries(qT_ref)
        produce_block0()

    init = (tuple(jnp.full((1, 2 * tq), NEG_BIG, F32) for _ in range(ATT_G)),
            tuple(jnp.ones((1, 2 * tq), F32) for _ in range(ATT_G)),
            tuple(mb_ref[g] for g in range(ATT_G)))
    carry = lax.fori_loop(0, n_full // 4, body, init)
    carry = lax.cond(n_full % 4 >= 2, lambda c: pair((n_full // 4) * 4, c), lambda c: c, carry)
    ms, alphas, _ = lax.cond(n_full % 2 == 1,
                             lambda c: half_step(n_full - 1, 0, *c), lambda c: c, carry)

    pv(load_v(jnp.maximum(n_full - 1, 0)), alphas)
    kpos = lax.broadcasted_iota(jnp.int32, (tk, 2 * tq), 0)
    qcol = lax.broadcasted_iota(jnp.int32, (tk, 2 * tq), 1)
    qpos = jnp.where(qcol >= tq, qcol - tq, qcol) + (qi * tq - n_full * tk)
    causal = kpos <= qpos
    par_last = n_full % 2
    load_queries(qTn_ref)
    k0_blks = load_k(0)
    v_blks = load_v(n_full)
    for g in range(ATT_G):
        sT = jnp.where(causal, s_ref[par_last, g], NEG_BIG)
        _, alpha = consume_head(g, n_full, sT, jnp.max(sT, axis=0, keepdims=True), ms[g])
        mb_ref[g] = produce_head(g, k0_blks[g], 0)
        pv_head(g, v_blks[g], alpha)

    lam = (jnp.exp(jnp.sum(lq1_ref[...] * lk1_ref[...], axis=-1, keepdims=True))
           - jnp.exp(jnp.sum(lq2_ref[...] * lk2_ref[...], axis=-1, keepdims=True))
           + LAMBDA_INIT)
    for g in range(ATT_G):
        acc = acc_ref[g]
        inv_l = 1.0 / acc[V_HEAD_DIM:V_HEAD_DIM + 1, :]
        num = acc[:V_HEAD_DIM, :] * inv_l
        oT = num[:, :tq] - lam * num[:, tq:]
        o = _rms(oT.T, g_ref[...]) * (1.0 - LAMBDA_INIT)
        o_ref[:, g * 128:(g + 1) * 128] = o.astype(BF16)


def _cast_view(w, n_chunks):
    cols = w.shape[-1]
    rows = w.size // cols
    while rows % (n_chunks * 16):
        assert cols % 256 == 0, w.shape
        cols //= 2
        rows *= 2
    return w.reshape(rows, cols)


def _attention(cparts, qT, k, vT, lq1, lk1, lq2, lk2, subln_g, cast_ws):
    S = k.shape[0]
    nq, nk = S // ATT_TQ, S // ATT_TK
    G = ATT_G
    n_steps = (N_HEADS // G) * nq
    vec = lambda h, i: (0, 0)
    once = pl.Buffered(1)
    views = [_cast_view(w, n_steps) for w in cast_ws]
    cast_specs = [pl.BlockSpec((v.shape[0] // n_steps, v.shape[1]), lambda h, i: (h * nq + i, 0))
                  for v in views]
    outs = pl.pallas_call(
        functools.partial(_attn_kernel, n_cast=len(views)),
        grid=(N_HEADS // G, nq),
        in_specs=[
            pl.BlockSpec(memory_space=pltpu.SMEM),
            pl.BlockSpec((G, None, 128, ATT_TQ), lambda h, i: (h, i, 0, 0)),
            pl.BlockSpec((G, None, 128, ATT_TQ), lambda h, i: (h, jnp.minimum(i + 1, nq - 1), 0, 0)),
            pl.BlockSpec((S, G * 128), lambda h, i: (0, h), pipeline_mode=once),
            pl.BlockSpec((G, nk, V_ROWS, ATT_TK), lambda h, i: (h, 0, 0, 0), pipeline_mode=once),
            pl.BlockSpec((1, QK_HEAD_DIM), vec),
            pl.BlockSpec((1, QK_HEAD_DIM), vec),
            pl.BlockSpec((1, QK_HEAD_DIM), vec),
            pl.BlockSpec((1, QK_HEAD_DIM), vec),
            pl.BlockSpec((1, V_HEAD_DIM), vec),
        ] + cast_specs,
        out_specs=[pl.BlockSpec((ATT_TQ, G * 128), lambda h, i: (i, h))] + cast_specs,
        out_shape=[jax.ShapeDtypeStruct((S, ATT_WIDTH), BF16)]
                  + [jax.ShapeDtypeStruct(v.shape, BF16) for v in views],
        scratch_shapes=[
            pltpu.VMEM((G, 256, 2 * ATT_TQ), BF16),
            pltpu.VMEM((G, V_ROWS, 2 * ATT_TQ), F32),
            pltpu.VMEM((G, ATT_TK, 2 * ATT_TQ), BF16),
            pltpu.VMEM((ATT_TK, 128), BF16),
            pltpu.VMEM((2, G, ATT_TK, 2 * ATT_TQ), F32),
            pltpu.VMEM((G, 1, 2 * ATT_TQ), F32),
        ],
        compiler_params=_params(("arbitrary", "arbitrary")),
        name="diff_attn",
    )(cparts, qT, qT, k, vT, lq1, lk1, lq2, lk2, subln_g, *views)
    return outs[0], [wb.reshape(w.shape) for wb, w in zip(outs[1:], cast_ws)]


def _mix_kernel(o_ref, u_ref, uprev_ref, gate_ref, x_ref, wg_ref, ps_ref, wa_ref, wp_ref, wo_ref,
                gm_ref, x1_ref, hm_ref, *, bm):
    i = pl.program_id(0)
    u = u_ref[...]
    halo = uprev_ref[...]
    ext = jnp.concatenate([jnp.where(i == 0, jnp.zeros_like(halo), halo), u], axis=0)
    t = i * bm + lax.broadcasted_iota(jnp.int32, (bm, 1), 0)
    pms = []
    a_chunks = []
    n_groups = len(POOL_WINDOWS)
    a_cols = D_MODEL // n_groups
    o = o_ref[...]
    for g, w in enumerate(POOL_WINDOWS):
        a_chunks.append(_dot(o, wa_ref[:, g * a_cols:(g + 1) * a_cols]))
        c0, c1 = g * POOL_GROUP_WIDTH, (g + 1) * POOL_GROUP_WIDTH
        ug = u[:, c0:c1]
        acc = ext[:, c0:c1]
        span = 1
        while span < w:
            acc = acc + pltpu.roll(acc, span, axis=0)
            span *= 2
        win = acc[POOL_HALO:, :]
        count = jnp.minimum(t + 1, w).astype(F32)
        pooled = win / count - ug
        pms.append(_dot(pooled.astype(BF16), wg_ref[g]))
    pm = (jnp.concatenate(pms, axis=-1) * ps_ref[...]).astype(BF16)
    p_branch = _dot(pm, wp_ref[...])
    a_branch = jnp.concatenate(a_chunks, axis=-1)
    gates = gate_ref[...]
    merged = gates[:, :D_MODEL].astype(F32) * a_branch + gates[:, D_MODEL:].astype(F32) * p_branch
    x1 = x_ref[...] + _dot(merged.astype(BF16), wo_ref[...])
    x1_ref[...] = x1
    hm_ref[...] = _rms(x1, gm_ref[...]).astype(BF16)


def _mix(o, u, gates, x, wg, pool_scale, wa, wp, wo, g_mlp, bm=256):
    S = x.shape[0]
    const2 = lambda i: (0, 0)
    once = pl.Buffered(1)
    halo_blocks = bm // POOL_HALO
    return pl.pallas_call(
        functools.partial(_mix_kernel, bm=bm),
        grid=(S // bm,),
        in_specs=[
            pl.BlockSpec((bm, ATT_WIDTH), lambda i: (i, 0)),
            pl.BlockSpec((bm, POOL_WIDTH), lambda i: (i, 0)),
            pl.BlockSpec((POOL_HALO, POOL_WIDTH), lambda i: (jnp.maximum(i * halo_blocks - 1, 0), 0)),
            pl.BlockSpec((bm, 2 * D_MODEL), lambda i: (i, 0)),
            pl.BlockSpec((bm, D_MODEL), lambda i: (i, 0)),
            pl.BlockSpec((len(POOL_WINDOWS), POOL_GROUP_WIDTH, POOL_GROUP_WIDTH), lambda i: (0, 0, 0),
                         pipeline_mode=once),
            pl.BlockSpec((1, POOL_WIDTH), const2),
            pl.BlockSpec((ATT_WIDTH, D_MODEL), const2, pipeline_mode=once),
            pl.BlockSpec((POOL_WIDTH, D_MODEL), const2, pipeline_mode=once),
            pl.BlockSpec((D_MODEL, D_MODEL), const2, pipeline_mode=once),
            pl.BlockSpec((1, D_MODEL), const2),
        ],
        out_specs=[
            pl.BlockSpec((bm, D_MODEL), lambda i: (i, 0)),
            pl.BlockSpec((bm, D_MODEL), lambda i: (i, 0)),
        ],
        out_shape=[
            jax.ShapeDtypeStruct((S, D_MODEL), F32),
            jax.ShapeDtypeStruct((S, D_MODEL), BF16),
        ],
        compiler_params=_params(("arbitrary",)),
        name="mix_out",
    )(o, u, u, gates, x, wg, pool_scale, wa, wp, wo, g_mlp)


def _mlp_kernel(hm_ref, wu_ref, wd_ref, d_ref):
    @pl.when(pl.program_id(1) == 0)
    def _():
        d_ref[...] = jnp.zeros_like(d_ref)

    a = jnp.maximum(_dot(hm_ref[...], wu_ref[...]), 0.0)
    d_ref[...] += _dot((a * a).astype(BF16), wd_ref[...])


def _mlp(hm, wu, wd, bm=1024, bf=1024):
    S = hm.shape[0]
    return pl.pallas_call(
        _mlp_kernel,
        grid=(S // bm, D_FF // bf),
        in_specs=[
            pl.BlockSpec((bm, D_MODEL), lambda i, f: (i, 0)),
            pl.BlockSpec((D_MODEL, bf), lambda i, f: (0, f)),
            pl.BlockSpec((bf, D_MODEL), lambda i, f: (f, 0)),
        ],
        out_specs=pl.BlockSpec((bm, D_MODEL), lambda i, f: (i, 0)),
        out_shape=jax.ShapeDtypeStruct((S, D_MODEL), F32),
        compiler_params=_params(("arbitrary", "arbitrary")),
        name="mlp",
    )(hm, wu, wd)


PLE_CHUNK = 512


def _ple_kernel(x1_ref, d_ref, p_ref, gp_ref, wgate_ref, wple_ref, gf_ref, out_ref, *, bm):
    n_chunks = D_MODEL // PLE_CHUNK
    halves = [slice(0, bm // 2), slice(bm // 2, bm)]

    def prenorm(rows):
        x2 = x1_ref[rows, :] + d_ref[rows, :]
        return x2, _rms(x2, gp_ref[...]).astype(BF16), p_ref[rows, :].astype(BF16)

    def chunk(state, c):
        x2, hp, pb = state
        cols = slice(c * PLE_CHUNK, (c + 1) * PLE_CHUNK)
        gate = _sigmoid(_dot(hp, wgate_ref[:, cols]))
        return x2[:, cols] + _dot(pb, wple_ref[:, cols]) * gate

    def finish(rows, x3):
        out_ref[rows, :] = _rms(jnp.concatenate(x3, axis=-1), gf_ref[...])

    a = prenorm(halves[0])
    xa = [chunk(a, 0)]
    b = prenorm(halves[1])
    xa += [chunk(a, c) for c in range(1, n_chunks)]
    xb = [chunk(b, 0)]
    finish(halves[0], xa)
    xb += [chunk(b, c) for c in range(1, n_chunks)]
    finish(halves[1], xb)


def _ple(x1, d, p, g_ple, wgate, wple, g_final, bm=512):
    S = x1.shape[0]
    const2 = lambda i: (0, 0)
    once = pl.Buffered(1)
    return pl.pallas_call(
        functools.partial(_ple_kernel, bm=bm),
        grid=(S // bm,),
        in_specs=[
            pl.BlockSpec((bm, D_MODEL), lambda i: (i, 0)),
            pl.BlockSpec((bm, D_MODEL), lambda i: (i, 0)),
            pl.BlockSpec((None, None, bm, PLE_DIM), lambda i: (0, 0, i, 0)),
            pl.BlockSpec((1, D_MODEL), const2),
            pl.BlockSpec((D_MODEL, D_MODEL), const2, pipeline_mode=once),
            pl.BlockSpec((PLE_DIM, D_MODEL), const2, pipeline_mode=once),
            pl.BlockSpec((1, D_MODEL), const2),
        ],
        out_specs=pl.BlockSpec((bm, D_MODEL), lambda i: (i, 0)),
        out_shape=jax.ShapeDtypeStruct((S, D_MODEL), F32),
        compiler_params=_params(("arbitrary",)),
        name="ple_final",
    )(x1, d, p, g_ple, wgate, wple, g_final)


def kernel(x, p, norm_mix_g, w_in, lambda_q1, lambda_k1, lambda_q2, lambda_k2, subln_g, pool_grp_w, pool_scale, w_attn_br, w_pool_br, w_out, norm_mlp_g, w_mlp_up, w_mlp_down, norm_ple_g, w_ple, w_ple_gate, final_norm_g):
    B, S, D = x.shape
    assert (B, S, D) == (1, SEQ, D_MODEL) and norm_mix_g.shape[0] == 1
    cparts = jnp.asarray(_alibi_parts())

    x2d = x[0]
    h = _prenorm(x2d, norm_mix_g)
    qT, k, vT, u, gates = _inproj(h, w_in[0])
    o, (wg, wa, wp, wo, wu, wd, wgate, wple) = _attention(
        cparts, qT, k, vT, lambda_q1, lambda_k1, lambda_q2, lambda_k2, subln_g,
        [pool_grp_w[0], w_attn_br[0], w_pool_br[0], w_out[0], w_mlp_up[0], w_mlp_down[0],
         w_ple_gate[0], w_ple[0]])
    x1, hm = _mix(o, u, gates, x2d, wg, pool_scale, wa, wp, wo, norm_mlp_g)
    d = _mlp(hm, wu, wd)
    out = _ple(x1, d, p, norm_ple_g, wgate, wple, final_norm_g.reshape(1, D_MODEL))
    return out[None]
```

```python
import functools
import math

import numpy as np
import jax
import jax.numpy as jnp
from jax import lax
from jax.experimental import pallas as pl
from jax.experimental.pallas import tpu as pltpu

F32 = jnp.float32
BF16 = jnp.bfloat16

D_MODEL = 2048
SEQ = 8192
N_HEADS = 8
V_HEAD_DIM = 128
QK_HEAD_DIM = 64
ATT_WIDTH = N_HEADS * V_HEAD_DIM
POOL_WIDTH = 1024
POOL_WINDOWS = (2, 4, 8, 16)
POOL_GROUP_WIDTH = POOL_WIDTH // len(POOL_WINDOWS)
POOL_HALO = 16
D_FF = 4 * D_MODEL
PLE_DIM = 256
NORM_EPS = 1e-6
LAMBDA_INIT = 0.8 - 0.6 * math.exp(-0.3 * 0)
NEG_BIG = -1e30
LOG2E = math.log2(math.e)
V_ROWS = V_HEAD_DIM + 16
BIAS_TERMS = 3


def _alibi_parts():
    import ml_dtypes
    rest = 2.0 ** (-8.0 * np.arange(1, N_HEADS + 1) / N_HEADS) * LOG2E
    parts = []
    for _ in range(BIAS_TERMS):
        piece = rest.astype(ml_dtypes.bfloat16).astype(np.float64)
        parts.append(piece)
        rest = rest - piece
    return np.stack(parts, axis=1).reshape(-1).astype(np.float32)

ATT_TQ = 256
ATT_TK = 512
ATT_G = 4
VMEM_LIMIT = 56 * 1024 * 1024
INPROJ_VMEM_LIMIT = 60 * 1024 * 1024


def _rms(xf, g):
    return xf * lax.rsqrt(jnp.mean(xf * xf, axis=-1, keepdims=True) + NORM_EPS) * g


def _dot(a, b):
    return jnp.dot(a, b, preferred_element_type=F32)


def _sigmoid(z):
    return 0.5 * jnp.tanh(0.5 * z) + 0.5


def _params(sem, vmem_limit=VMEM_LIMIT):
    return pltpu.CompilerParams(dimension_semantics=sem, vmem_limit_bytes=vmem_limit)


def _prenorm_kernel(x_ref, g_ref, h_ref):
    h_ref[...] = _rms(x_ref[...], g_ref[...]).astype(BF16)


def _prenorm(x, g, bm=512):
    S = x.shape[0]
    return pl.pallas_call(
        _prenorm_kernel,
        grid=(S // bm,),
        in_specs=[pl.BlockSpec((bm, D_MODEL), lambda i: (i, 0)),
                  pl.BlockSpec((1, D_MODEL), lambda i: (0, 0))],
        out_specs=pl.BlockSpec((bm, D_MODEL), lambda i: (i, 0)),
        out_shape=jax.ShapeDtypeStruct((S, D_MODEL), BF16),
        compiler_params=_params(("arbitrary",)),
        name="prenorm",
    )(x, g)


IN_PHASE_COLS = 1024
IN_CHUNK = 256
N_IN_PHASES = 8
GATE_PHASE0 = 4


def _inproj_kernel(h_ref, w_ref, qT_ref, k_ref, vT_ref, u_ref, gate_ref, wb_ref, *, bm):
    j = pl.program_id(0)
    i = pl.program_id(1)
    half = IN_PHASE_COLS // 2

    @pl.when(jnp.logical_and(i == 0, j == 1))
    def _():
        for hh in range(N_HEADS):
            for m in range(2):
                src = m * half + hh * QK_HEAD_DIM
                dst = hh * 128 + m * QK_HEAD_DIM
                wb_ref[:, dst:dst + QK_HEAD_DIM] = w_ref[:, src:src + QK_HEAD_DIM].astype(BF16)

    def chunks(regrouped=False):
        h = h_ref[...]
        for c in range(IN_PHASE_COLS // IN_CHUNK):
            cols = slice(c * IN_CHUNK, (c + 1) * IN_CHUNK)
            w = wb_ref[:, cols] if regrouped else w_ref[:, cols].astype(BF16)
            yield c, _dot(h, w)

    @pl.when(j == 0)
    def _():
        heads_per_chunk = IN_CHUNK // QK_HEAD_DIM
        for c, z in chunks():
            zT = (z * (QK_HEAD_DIM ** -0.5 * LOG2E)).T.astype(BF16)
            m = (c * IN_CHUNK) // half
            h0 = ((c * IN_CHUNK) % half) // QK_HEAD_DIM
            for hl in range(heads_per_chunk):
                for r in range(bm // ATT_TQ):
                    qT_ref[h0 + hl, r, m * QK_HEAD_DIM:(m + 1) * QK_HEAD_DIM, :] = (
                        zT[hl * QK_HEAD_DIM:(hl + 1) * QK_HEAD_DIM, r * ATT_TQ:(r + 1) * ATT_TQ])

    @pl.when(j == 1)
    def _():
        for c, z in chunks(regrouped=True):
            k_ref[:, c * IN_CHUNK:(c + 1) * IN_CHUNK] = z.astype(BF16)

    @pl.when(j == 2)
    def _():
        heads_per_chunk = IN_CHUNK // V_HEAD_DIM
        ones = jnp.ones((V_ROWS - V_HEAD_DIM, ATT_TK), BF16)
        for c, z in chunks():
            zT = z.T.astype(BF16)
            for hl in range(heads_per_chunk):
                hh = c * heads_per_chunk + hl
                for r in range(bm // ATT_TK):
                    vT_ref[hh, r, 0:V_HEAD_DIM, :] = (
                        zT[hl * V_HEAD_DIM:(hl + 1) * V_HEAD_DIM, r * ATT_TK:(r + 1) * ATT_TK])
                    vT_ref[hh, r, V_HEAD_DIM:V_ROWS, :] = ones

    @pl.when(j == 3)
    def _():
        for c, z in chunks():
            u_ref[:, c * IN_CHUNK:(c + 1) * IN_CHUNK] = z

    @pl.when(j >= GATE_PHASE0)
    def _():
        for c, z in chunks():
            gate_ref[:, c * IN_CHUNK:(c + 1) * IN_CHUNK] = _sigmoid(z).astype(BF16)


def _inproj(h, w_in, bm=1024):
    S = h.shape[0]
    n_rows = S // bm
    nq, nk = S // ATT_TQ, S // ATT_TK

    def rows_in_phase(phase):
        def f(j, i):
            return jnp.where(j == phase, i, jnp.where(j < phase, 0, n_rows - 1))
        return f

    rq, rk, rv, ru = (rows_in_phase(ph) for ph in range(4))
    return pl.pallas_call(
        functools.partial(_inproj_kernel, bm=bm),
        grid=(N_IN_PHASES, n_rows),
        in_specs=[
            pl.BlockSpec((bm, D_MODEL), lambda j, i: (i, 0)),
            pl.BlockSpec((D_MODEL, IN_PHASE_COLS), lambda j, i: (0, j)),
        ],
        out_specs=[
            pl.BlockSpec((N_HEADS, bm // ATT_TQ, 128, ATT_TQ), lambda j, i: (0, rq(j, i), 0, 0)),
            pl.BlockSpec((bm, 1024), lambda j, i: (rk(j, i), 0)),
            pl.BlockSpec((N_HEADS, bm // ATT_TK, V_ROWS, ATT_TK), lambda j, i: (0, rv(j, i), 0, 0)),
            pl.BlockSpec((bm, POOL_WIDTH), lambda j, i: (ru(j, i), 0)),
            pl.BlockSpec((bm, IN_PHASE_COLS),
                         lambda j, i: (jnp.where(j >= GATE_PHASE0, i, 0), jnp.maximum(j - GATE_PHASE0, 0))),
        ],
        out_shape=[
            jax.ShapeDtypeStruct((N_HEADS, nq, 128, ATT_TQ), BF16),
            jax.ShapeDtypeStruct((S, 1024), BF16),
            jax.ShapeDtypeStruct((N_HEADS, nk, V_ROWS, ATT_TK), BF16),
            jax.ShapeDtypeStruct((S, POOL_WIDTH), F32),
            jax.ShapeDtypeStruct((S, 2 * D_MODEL), BF16),
        ],
        scratch_shapes=[pltpu.VMEM((D_MODEL, IN_PHASE_COLS), BF16)],
        compiler_params=_params(("arbitrary", "arbitrary"), vmem_limit=INPROJ_VMEM_LIMIT),
        name="inproj",
    )(h, w_in)


def _attn_kernel(cpart_ref, qT_ref, qTn_ref, k_ref, vT_ref, lq1_ref, lk1_ref, lq2_ref, lk2_ref,
                 g_ref, *rest, n_cast):
    cast_in, (o_ref, *cast_out) = rest[:n_cast], rest[n_cast:2 * n_cast + 1]
    qs_ref, acc_ref, p_ref, feat_ref, s_ref, mb_ref = rest[2 * n_cast + 1:]
    hg = pl.program_id(0)
    qi = pl.program_id(1)
    tq, tk = ATT_TQ, ATT_TK

    for w_ref, wb_ref in zip(cast_in, cast_out):
        wb_ref[...] = w_ref[...].astype(BF16)

    parts = [[cpart_ref[(hg * ATT_G + g) * BIAS_TERMS + t] for t in range(BIAS_TERMS)]
             for g in range(ATT_G)]
    slopes = [sum(pt[1:], pt[0]) for pt in parts]

    def load_queries(src_ref):
        row = lax.broadcasted_iota(jnp.int32, (128, tq), 0)
        for g in range(ATT_G):
            qT = src_ref[g]
            zero = jnp.zeros_like(qT)
            qs_ref[g, 0:128, :tq] = jnp.where(row < QK_HEAD_DIM, qT, zero)
            qs_ref[g, 0:128, tq:] = jnp.where(row >= QK_HEAD_DIM, qT, zero)

    def load_k(j):
        start = pl.multiple_of(j * tk, tk)
        return [k_ref[pl.ds(start, tk), g * 128:(g + 1) * 128] for g in range(ATT_G)]

    def load_v(j):
        return [vT_ref[g, j] for g in range(ATT_G)]

    def produce_head(g, k_blk, par):
        k_aug = jnp.concatenate([k_blk, feat_ref[...]], axis=1)
        sT = _dot(k_aug, qs_ref[g])
        s_ref[par, g] = sT
        return jnp.max(sT, axis=0, keepdims=True)

    def produce(k_blks, par):
        return tuple(produce_head(g, k_blks[g], par) for g in range(ATT_G))

    def pv_head(g, v_blk, alpha):
        acc_ref[g] = alpha * acc_ref[g] + _dot(v_blk, p_ref[g])

    def pv(v_blks, alphas):
        for g in range(ATT_G):
            pv_head(g, v_blks[g], alphas[g])

    def consume_head(g, j, sT, mblk, m):
        off = slopes[g] * (j * tk - qi * tq).astype(F32)
        m_new = jnp.maximum(m, mblk + off)
        p_ref[g] = jnp.exp2(sT - (m_new - off)).astype(BF16)
        return m_new, jnp.exp2(m - m_new)

    def consume(j, sTs, mblks, ms):
        outs = [consume_head(g, j, sTs[g], mblks[g], ms[g]) for g in range(ATT_G)]
        return tuple(o[0] for o in outs), tuple(o[1] for o in outs)

    def half_step(j, par, ms, alphas, mblks):
        k_blks = load_k(j + 1)
        v_blks = load_v(jnp.maximum(j - 1, 0))
        mblks_next = produce(k_blks, 1 - par)
        pv(v_blks, alphas)
        ms, alphas = consume(j, [s_ref[par, g] for g in range(ATT_G)], mblks, ms)
        return ms, alphas, mblks_next

    n_full = (qi * tq) // tk

    def pair(j, carry):
        carry = half_step(j, 0, *carry)
        return half_step(j + 1, 1, *carry)

    def body(t, carry):
        return pair(4 * t + 2, pair(4 * t, carry))

    def produce_block0():
        for g, mblk in enumerate(produce(load_k(0), 0)):
            mb_ref[g] = mblk

    @pl.when(qi == 0)
    def _():
        krow = lax.broadcasted_iota(jnp.int32, (tk, 128), 0)
        klane = lax.broadcasted_iota(jnp.int32, (tk, 128), 1)
        feat = jnp.where(klane < BIAS_TERMS, krow >> 4,
                         jnp.where(klane < 2 * BIAS_TERMS, krow & 15, 0))
        feat_ref[...] = feat.astype(F32).astype(BF16)
        brow = lax.broadcasted_iota(jnp.int32, (128, 2 * tq), 0)
        for g in range(ATT_G):
            bias_rows = jnp.zeros((128, 2 * tq), F32)
            for t in range(BIAS_TERMS):
                bias_rows = jnp.where(brow == t, 16.0 * parts[g][t], bias_rows)
                bias_rows = jnp.where(brow == BIAS_TERMS + t, parts[g][t], bias_rows)
            qs_ref[g, 128:256, :] = bias_rows.astype(BF16)
        load_queries(qT_ref)
        produce_block0()
        acc_ref[...] = jnp.zeros_like(acc_ref)
        p_ref[...] = jnp.zeros_like(p_ref)

    init = (tuple(jnp.full((1, 2 * tq), NEG_BIG, F32) for _ in range(ATT_G)),
            tuple(jnp.zeros((1, 2 * tq), F32) for _ in range(ATT_G)),
            tuple(mb_ref[g] for g in range(ATT_G)))
    carry = lax.fori_loop(0, n_full // 4, body, init)
    carry = lax.cond(n_full % 4 >= 2, lambda c: pair((n_full // 4) * 4, c), lambda c: c, carry)
    ms, alphas, _ = lax.cond(n_full % 2 == 1,
                             lambda c: half_step(n_full - 1, 0, *c), lambda c: c, carry)

    pv(load_v(jnp.maximum(n_full - 1, 0)), alphas)
    kpos = lax.broadcasted_iota(jnp.int32, (tk, 2 * tq), 0)
    qcol = lax.broadcasted_iota(jnp.int32, (tk, 2 * tq), 1)
    qpos = jnp.where(qcol >= tq, qcol - tq, qcol) + (qi * tq - n_full * tk)
    causal = kpos <= qpos
    par_last = n_full % 2
    load_queries(qTn_ref)
    k0_blks = load_k(0)
    v_blks = load_v(n_full)
    for g in range(ATT_G):
        sT = jnp.where(causal, s_ref[par_last, g], NEG_BIG)
        _, alpha = consume_head(g, n_full, sT, jnp.max(sT, axis=0, keepdims=True), ms[g])
        mb_ref[g] = produce_head(g, k0_blks[g], 0)
        pv_head(g, v_blks[g], alpha)

    lam = (jnp.exp(jnp.sum(lq1_ref[...] * lk1_ref[...], axis=-1, keepdims=True))
           - jnp.exp(jnp.sum(lq2_ref[...] * lk2_ref[...], axis=-1, keepdims=True))
           + LAMBDA_INIT)
    for g in range(ATT_G):
        acc = acc_ref[g]
        inv_l = 1.0 / acc[V_HEAD_DIM:V_HEAD_DIM + 1, :]
        num = acc[:V_HEAD_DIM, :] * inv_l
        oT = num[:, :tq] - lam * num[:, tq:]
        o = _rms(oT.T, g_ref[...]) * (1.0 - LAMBDA_INIT)
        o_ref[:, g * 128:(g + 1) * 128] = o.astype(BF16)


def _cast_view(w, n_chunks):
    cols = w.shape[-1]
    rows = w.size // cols
    while rows % (n_chunks * 16):
        assert cols % 256 == 0, w.shape
        cols //= 2
        rows *= 2
    return w.reshape(rows, cols)


def _attention(cparts, qT, k, vT, lq1, lk1, lq2, lk2, subln_g, cast_ws):
    S = k.shape[0]
    nq, nk = S // ATT_TQ, S // ATT_TK
    G = ATT_G
    n_steps = (N_HEADS // G) * nq
    vec = lambda h, i: (0, 0)
    once = pl.Buffered(1)
    views = [_cast_view(w, n_steps) for w in cast_ws]
    cast_specs = [pl.BlockSpec((v.shape[0] // n_steps, v.shape[1]), lambda h, i: (h * nq + i, 0))
                  for v in views]
    outs = pl.pallas_call(
        functools.partial(_attn_kernel, n_cast=len(views)),
        grid=(N_HEADS // G, nq),
        in_specs=[
            pl.BlockSpec(memory_space=pltpu.SMEM),
            pl.BlockSpec((G, None, 128, ATT_TQ), lambda h, i: (h, i, 0, 0)),
            pl.BlockSpec((G, None, 128, ATT_TQ), lambda h, i: (h, jnp.minimum(i + 1, nq - 1), 0, 0)),
            pl.BlockSpec((S, G * 128), lambda h, i: (0, h), pipeline_mode=once),
            pl.BlockSpec((G, nk, V_ROWS, ATT_TK), lambda h, i: (h, 0, 0, 0), pipeline_mode=once),
            pl.BlockSpec((1, QK_HEAD_DIM), vec),
            pl.BlockSpec((1, QK_HEAD_DIM), vec),
            pl.BlockSpec((1, QK_HEAD_DIM), vec),
            pl.BlockSpec((1, QK_HEAD_DIM), vec),
            pl.BlockSpec((1, V_HEAD_DIM), vec),
        ] + cast_specs,
        out_specs=[pl.BlockSpec((ATT_TQ, G * 128), lambda h, i: (i, h))] + cast_specs,
        out_shape=[jax.ShapeDtypeStruct((S, ATT_WIDTH), BF16)]
                  + [jax.ShapeDtypeStruct(v.shape, BF16) for v in views],
        scratch_shapes=[
            pltpu.VMEM((G, 256, 2 * ATT_TQ), BF16),
            pltpu.VMEM((G, V_ROWS, 2 * ATT_TQ), F32),
            pltpu.VMEM((G, ATT_TK, 2 * ATT_TQ), BF16),
            pltpu.VMEM((ATT_TK, 128), BF16),
            pltpu.VMEM((2, G, ATT_TK, 2 * ATT_TQ), F32),
            pltpu.VMEM((G, 1, 2 * ATT_TQ), F32),
        ],
        compiler_params=_params(("arbitrary", "arbitrary")),
        name="diff_attn",
    )(cparts, qT, qT, k, vT, lq1, lk1, lq2, lk2, subln_g, *views)
    return outs[0], [wb.reshape(w.shape) for wb, w in zip(outs[1:], cast_ws)]


def _mix_kernel(o_ref, u_ref, uprev_ref, gate_ref, x_ref, wg_ref, ps_ref, wa_ref, wp_ref, wo_ref,
                gm_ref, x1_ref, hm_ref, *, bm):
    i = pl.program_id(0)
    u = u_ref[...]
    halo = uprev_ref[...]
    ext = jnp.concatenate([jnp.where(i == 0, jnp.zeros_like(halo), halo), u], axis=0)
    t = i * bm + lax.broadcasted_iota(jnp.int32, (bm, 1), 0)
    pms = []
    a_chunks = []
    n_groups = len(POOL_WINDOWS)
    a_cols = D_MODEL // n_groups
    o = o_ref[...]
    for g, w in enumerate(POOL_WINDOWS):
        a_chunks.append(_dot(o, wa_ref[:, g * a_cols:(g + 1) * a_cols]))
        c0, c1 = g * POOL_GROUP_WIDTH, (g + 1) * POOL_GROUP_WIDTH
        ug = u[:, c0:c1]
        acc = ext[:, c0:c1]
        span = 1
        while span < w:
            acc = acc + pltpu.roll(acc, span, axis=0)
            span *= 2
        win = acc[POOL_HALO:, :]
        count = jnp.minimum(t + 1, w).astype(F32)
        pooled = win / count - ug
        pms.append(_dot(pooled.astype(BF16), wg_ref[g]))
    pm = (jnp.concatenate(pms, axis=-1) * ps_ref[...]).astype(BF16)
    p_branch = _dot(pm, wp_ref[...])
    a_branch = jnp.concatenate(a_chunks, axis=-1)
    gates = gate_ref[...]
    merged = gates[:, :D_MODEL].astype(F32) * a_branch + gates[:, D_MODEL:].astype(F32) * p_branch
    x1 = x_ref[...] + _dot(merged.astype(BF16), wo_ref[...])
    x1_ref[...] = x1
    hm_ref[...] = _rms(x1, gm_ref[...]).astype(BF16)


def _mix(o, u, gates, x, wg, pool_scale, wa, wp, wo, g_mlp, bm=256):
    S = x.shape[0]
    const2 = lambda i: (0, 0)
    once = pl.Buffered(1)
    halo_blocks = bm // POOL_HALO
    return pl.pallas_call(
        functools.partial(_mix_kernel, bm=bm),
        grid=(S // bm,),
        in_specs=[
            pl.BlockSpec((bm, ATT_WIDTH), lambda i: (i, 0)),
            pl.BlockSpec((bm, POOL_WIDTH), lambda i: (i, 0)),
            pl.BlockSpec((POOL_HALO, POOL_WIDTH), lambda i: (jnp.maximum(i * halo_blocks - 1, 0), 0)),
            pl.BlockSpec((bm, 2 * D_MODEL), lambda i: (i, 0)),
            pl.BlockSpec((bm, D_MODEL), lambda i: (i, 0)),
            pl.BlockSpec((len(POOL_WINDOWS), POOL_GROUP_WIDTH, POOL_GROUP_WIDTH), lambda i: (0, 0, 0),
                         pipeline_mode=once),
            pl.BlockSpec((1, POOL_WIDTH), const2),
            pl.BlockSpec((ATT_WIDTH, D_MODEL), const2, pipeline_mode=once),
            pl.BlockSpec((POOL_WIDTH, D_MODEL), const2, pipeline_mode=once),
            pl.BlockSpec((D_MODEL, D_MODEL), const2, pipeline_mode=once),
            pl.BlockSpec((1, D_MODEL), const2),
        ],
        out_specs=[
            pl.BlockSpec((bm, D_MODEL), lambda i: (i, 0)),
            pl.BlockSpec((bm, D_MODEL), lambda i: (i, 0)),
        ],
        out_shape=[
            jax.ShapeDtypeStruct((S, D_MODEL), F32),
            jax.ShapeDtypeStruct((S, D_MODEL), BF16),
        ],
        compiler_params=_params(("arbitrary",)),
        name="mix_out",
    )(o, u, u, gates, x, wg, pool_scale, wa, wp, wo, g_mlp)


def _mlp_kernel(hm_ref, wu_ref, wd_ref, d_ref):
    @pl.when(pl.program_id(1) == 0)
    def _():
        d_ref[...] = jnp.zeros_like(d_ref)

    a = jnp.maximum(_dot(hm_ref[...], wu_ref[...]), 0.0)
    d_ref[...] += _dot((a * a).astype(BF16), wd_ref[...])


def _mlp(hm, wu, wd, bm=1024, bf=1024):
    S = hm.shape[0]
    return pl.pallas_call(
        _mlp_kernel,
        grid=(S // bm, D_FF // bf),
        in_specs=[
            pl.BlockSpec((bm, D_MODEL), lambda i, f: (i, 0)),
            pl.BlockSpec((D_MODEL, bf), lambda i, f: (0, f)),
            pl.BlockSpec((bf, D_MODEL), lambda i, f: (f, 0)),
        ],
        out_specs=pl.BlockSpec((bm, D_MODEL), lambda i, f: (i, 0)),
        out_shape=jax.ShapeDtypeStruct((S, D_MODEL), F32),
        compiler_params=_params(("arbitrary", "arbitrary")),
        name="mlp",
    )(hm, wu, wd)


PLE_CHUNK = 512


def _ple_kernel(x1_ref, d_ref, p_ref, gp_ref, wgate_ref, wple_ref, gf_ref, out_ref, *, bm):
    n_chunks = D_MODEL // PLE_CHUNK
    halves = [slice(0, bm // 2), slice(bm // 2, bm)]

    def prenorm(rows):
        x2 = x1_ref[rows, :] + d_ref[rows, :]
        return x2, _rms(x2, gp_ref[...]).astype(BF16), p_ref[rows, :].astype(BF16)

    def chunk(state, c):
        x2, hp, pb = state
        cols = slice(c * PLE_CHUNK, (c + 1) * PLE_CHUNK)
        gate = _sigmoid(_dot(hp, wgate_ref[:, cols]))
        return x2[:, cols] + _dot(pb, wple_ref[:, cols]) * gate

    def finish(rows, x3):
        out_ref[rows, :] = _rms(jnp.concatenate(x3, axis=-1), gf_ref[...])

    a = prenorm(halves[0])
    xa = [chunk(a, 0)]
    b = prenorm(halves[1])
    xa += [chunk(a, c) for c in range(1, n_chunks)]
    xb = [chunk(b, 0)]
    finish(halves[0], xa)
    xb += [chunk(b, c) for c in range(1, n_chunks)]
    finish(halves[1], xb)


def _ple(x1, d, p, g_ple, wgate, wple, g_final, bm=512):
    S = x1.shape[0]
    const2 = lambda i: (0, 0)
    once = pl.Buffered(1)
    return pl.pallas_call(
        functools.partial(_ple_kernel, bm=bm),
        grid=(S // bm,),
        in_specs=[
            pl.BlockSpec((bm, D_MODEL), lambda i: (i, 0)),
            pl.BlockSpec((bm, D_MODEL), lambda i: (i, 0)),
            pl.BlockSpec((None, None, bm, PLE_DIM), lambda i: (0, 0, i, 0)),
            pl.BlockSpec((1, D_MODEL), const2),
            pl.BlockSpec((D_MODEL, D_MODEL), const2, pipeline_mode=once),
            pl.BlockSpec((PLE_DIM, D_MODEL), const2, pipeline_mode=once),
            pl.BlockSpec((1, D_MODEL), const2),
        ],
        out_specs=pl.BlockSpec((bm, D_MODEL), lambda i: (i, 0)),
        out_shape=jax.ShapeDtypeStruct((S, D_MODEL), F32),
        compiler_params=_params(("arbitrary",)),
        name="ple_final",
    )(x1, d, p, g_ple, wgate, wple, g_final)


def kernel(x, p, norm_mix_g, w_in, lambda_q1, lambda_k1, lambda_q2, lambda_k2, subln_g, pool_grp_w, pool_scale, w_attn_br, w_pool_br, w_out, norm_mlp_g, w_mlp_up, w_mlp_down, norm_ple_g, w_ple, w_ple_gate, final_norm_g):
    B, S, D = x.shape
    assert (B, S, D) == (1, SEQ, D_MODEL) and norm_mix_g.shape[0] == 1
    cparts = jnp.asarray(_alibi_parts())

    x2d = x[0]
    h = _prenorm(x2d, norm_mix_g)
    qT, k, vT, u, gates = _inproj(h, w_in[0])
    o, (wg, wa, wp, wo, wu, wd, wgate, wple) = _attention(
        cparts, qT, k, vT, lambda_q1, lambda_k1, lambda_q2, lambda_k2, subln_g,
        [pool_grp_w[0], w_attn_br[0], w_pool_br[0], w_out[0], w_mlp_up[0], w_mlp_down[0],
         w_ple_gate[0], w_ple[0]])
    x1, hm = _mix(o, u, gates, x2d, wg, pool_scale, wa, wp, wo, norm_mlp_g)
    d = _mlp(hm, wu, wd)
    out = _ple(x1, d, p, norm_ple_g, wgate, wple, final_norm_g.reshape(1, D_MODEL))
    return out[None]
```

```python
import functools
import math

import numpy as np
import jax
import jax.numpy as jnp
from jax import lax
from jax.experimental import pallas as pl
from jax.experimental.pallas import tpu as pltpu

F32 = jnp.float32
BF16 = jnp.bfloat16

D_MODEL = 2048
SEQ = 8192
N_HEADS = 8
V_HEAD_DIM = 128
QK_HEAD_DIM = 64
ATT_WIDTH = N_HEADS * V_HEAD_DIM
POOL_WIDTH = 1024
POOL_WINDOWS = (2, 4, 8, 16)
POOL_GROUP_WIDTH = POOL_WIDTH // len(POOL_WINDOWS)
POOL_HALO = 16
D_FF = 4 * D_MODEL
PLE_DIM = 256
NORM_EPS = 1e-6
LAMBDA_INIT = 0.8 - 0.6 * math.exp(-0.3 * 0)
NEG_BIG = -1e30
LOG2E = math.log2(math.e)
V_ROWS = V_HEAD_DIM + 16
BIAS_TERMS = 3


def _alibi_parts():
    import ml_dtypes
    rest = 2.0 ** (-8.0 * np.arange(1, N_HEADS + 1) / N_HEADS) * LOG2E
    parts = []
    for _ in range(BIAS_TERMS):
        piece = rest.astype(ml_dtypes.bfloat16).astype(np.float64)
        parts.append(piece)
        rest = rest - piece
    return np.stack(parts, axis=1).reshape(-1).astype(np.float32)

ATT_TQ = 256
ATT_TK = 512
ATT_G = 4
VMEM_LIMIT = 56 * 1024 * 1024
INPROJ_VMEM_LIMIT = 60 * 1024 * 1024


def _rms(xf, g):
    return xf * lax.rsqrt(jnp.mean(xf * xf, axis=-1, keepdims=True) + NORM_EPS) * g


def _dot(a, b):
    return jnp.dot(a, b, preferred_element_type=F32)


def _sigmoid(z):
    return 0.5 * jnp.tanh(0.5 * z) + 0.5


def _params(sem, vmem_limit=VMEM_LIMIT):
    return pltpu.CompilerParams(dimension_semantics=sem, vmem_limit_bytes=vmem_limit)


def _prenorm_kernel(x_ref, g_ref, h_ref):
    h_ref[...] = _rms(x_ref[...], g_ref[...]).astype(BF16)


def _prenorm(x, g, bm=512):
    S = x.shape[0]
    return pl.pallas_call(
        _prenorm_kernel,
        grid=(S // bm,),
        in_specs=[pl.BlockSpec((bm, D_MODEL), lambda i: (i, 0)),
                  pl.BlockSpec((1, D_MODEL), lambda i: (0, 0))],
        out_specs=pl.BlockSpec((bm, D_MODEL), lambda i: (i, 0)),
        out_shape=jax.ShapeDtypeStruct((S, D_MODEL), BF16),
        compiler_params=_params(("arbitrary",)),
        name="prenorm",
    )(x, g)


IN_PHASE_COLS = 1024
IN_CHUNK = 256
N_IN_PHASES = 8
GATE_PHASE0 = 4


def _inproj_kernel(h_ref, w_ref, qT_ref, k_ref, vT_ref, u_ref, gate_ref, wb_ref, *, bm):
    j = pl.program_id(0)
    i = pl.program_id(1)
    half = IN_PHASE_COLS // 2

    @pl.when(jnp.logical_and(i == 0, j == 1))
    def _():
        for hh in range(N_HEADS):
            for m in range(2):
                src = m * half + hh * QK_HEAD_DIM
                dst = hh * 128 + m * QK_HEAD_DIM
                wb_ref[:, dst:dst + QK_HEAD_DIM] = w_ref[:, src:src + QK_HEAD_DIM].astype(BF16)

    def chunks(regrouped=False):
        h = h_ref[...]
        for c in range(IN_PHASE_COLS // IN_CHUNK):
            cols = slice(c * IN_CHUNK, (c + 1) * IN_CHUNK)
            w = wb_ref[:, cols] if regrouped else w_ref[:, cols].astype(BF16)
            yield c, _dot(h, w)

    @pl.when(j == 0)
    def _():
        heads_per_chunk = IN_CHUNK // QK_HEAD_DIM
        for c, z in chunks():
            zT = (z * (QK_HEAD_DIM ** -0.5 * LOG2E)).T.astype(BF16)
            m = (c * IN_CHUNK) // half
            h0 = ((c * IN_CHUNK) % half) // QK_HEAD_DIM
            for hl in range(heads_per_chunk):
                for r in range(bm // ATT_TQ):
                    qT_ref[h0 + hl, r, m * QK_HEAD_DIM:(m + 1) * QK_HEAD_DIM, :] = (
                        zT[hl * QK_HEAD_DIM:(hl + 1) * QK_HEAD_DIM, r * ATT_TQ:(r + 1) * ATT_TQ])

    @pl.when(j == 1)
    def _():
        for c, z in chunks(regrouped=True):
            k_ref[:, c * IN_CHUNK:(c + 1) * IN_CHUNK] = z.astype(BF16)

    @pl.when(j == 2)
    def _():
        heads_per_chunk = IN_CHUNK // V_HEAD_DIM
        ones = jnp.ones((V_ROWS - V_HEAD_DIM, ATT_TK), BF16)
        for c, z in chunks():
            zT = z.T.astype(BF16)
            for hl in range(heads_per_chunk):
                hh = c * heads_per_chunk + hl
                for r in range(bm // ATT_TK):
                    vT_ref[hh, r, 0:V_HEAD_DIM, :] = (
                        zT[hl * V_HEAD_DIM:(hl + 1) * V_HEAD_DIM, r * ATT_TK:(r + 1) * ATT_TK])
                    vT_ref[hh, r, V_HEAD_DIM:V_ROWS, :] = ones

    @pl.when(j == 3)
    def _():
        for c, z in chunks():
            u_ref[:, c * IN_CHUNK:(c + 1) * IN_CHUNK] = z

    @pl.when(j >= GATE_PHASE0)
    def _():
        for c, z in chunks():
            gate_ref[:, c * IN_CHUNK:(c + 1) * IN_CHUNK] = _sigmoid(z).astype(BF16)


def _inproj(h, w_in, bm=1024):
    S = h.shape[0]
    n_rows = S // bm
    nq, nk = S // ATT_TQ, S // ATT_TK

    def rows_in_phase(phase):
        def f(j, i):
            return jnp.where(j == phase, i, jnp.where(j < phase, 0, n_rows - 1))
        return f

    rq, rk, rv, ru = (rows_in_phase(ph) for ph in range(4))
    return pl.pallas_call(
        functools.partial(_inproj_kernel, bm=bm),
        grid=(N_IN_PHASES, n_rows),
        in_specs=[
            pl.BlockSpec((bm, D_MODEL), lambda j, i: (i, 0)),
            pl.BlockSpec((D_MODEL, IN_PHASE_COLS), lambda j, i: (0, j)),
        ],
        out_specs=[
            pl.BlockSpec((N_HEADS, bm // ATT_TQ, 128, ATT_TQ), lambda j, i: (0, rq(j, i), 0, 0)),
            pl.BlockSpec((bm, 1024), lambda j, i: (rk(j, i), 0)),
            pl.BlockSpec((N_HEADS, bm // ATT_TK, V_ROWS, ATT_TK), lambda j, i: (0, rv(j, i), 0, 0)),
            pl.BlockSpec((bm, POOL_WIDTH), lambda j, i: (ru(j, i), 0)),
            pl.BlockSpec((bm, IN_PHASE_COLS),
                         lambda j, i: (jnp.where(j >= GATE_PHASE0, i, 0), jnp.maximum(j - GATE_PHASE0, 0))),
        ],
        out_shape=[
            jax.ShapeDtypeStruct((N_HEADS, nq, 128, ATT_TQ), BF16),
            jax.ShapeDtypeStruct((S, 1024), BF16),
            jax.ShapeDtypeStruct((N_HEADS, nk, V_ROWS, ATT_TK), BF16),
            jax.ShapeDtypeStruct((S, POOL_WIDTH), F32),
            jax.ShapeDtypeStruct((S, 2 * D_MODEL), BF16),
        ],
        scratch_shapes=[pltpu.VMEM((D_MODEL, IN_PHASE_COLS), BF16)],
        compiler_params=_params(("arbitrary", "arbitrary"), vmem_limit=INPROJ_VMEM_LIMIT),
        name="inproj",
    )(h, w_in)


def _attn_kernel(cpart_ref, qT_ref, qTn_ref, k_ref, vT_ref, lq1_ref, lk1_ref, lq2_ref, lk2_ref,
                 g_ref, *rest, n_cast):
    cast_in, (o_ref, *cast_out) = rest[:n_cast], rest[n_cast:2 * n_cast + 1]
    qs_ref, acc_ref, p_ref, feat_ref, s_ref, mb_ref = rest[2 * n_cast + 1:]
    hg = pl.program_id(0)
    qi = pl.program_id(1)
    tq, tk = ATT_TQ, ATT_TK

    for w_ref, wb_ref in zip(cast_in, cast_out):
        wb_ref[...] = w_ref[...].astype(BF16)

    parts = [[cpart_ref[(hg * ATT_G + g) * BIAS_TERMS + t] for t in range(BIAS_TERMS)]
             for g in range(ATT_G)]
    slopes = [sum(pt[1:], pt[0]) for pt in parts]

    def load_queries(src_ref):
        row = lax.broadcasted_iota(jnp.int32, (128, tq), 0)
        for g in range(ATT_G):
            qT = src_ref[g]
            zero = jnp.zeros_like(qT)
            qs_ref[g, 0:128, :tq] = jnp.where(row < QK_HEAD_DIM, qT, zero)
            qs_ref[g, 0:128, tq:] = jnp.where(row >= QK_HEAD_DIM, qT, zero)

    def load_k(j):
        start = pl.multiple_of(j * tk, tk)
        return [k_ref[pl.ds(start, tk), g * 128:(g + 1) * 128] for g in range(ATT_G)]

    def load_v(j):
        return [vT_ref[g, j] for g in range(ATT_G)]

    def produce_head(g, k_blk, par):
        k_aug = jnp.concatenate([k_blk, feat_ref[...]], axis=1)
        sT = _dot(k_aug, qs_ref[g])
        s_ref[par, g] = sT
        return jnp.max(sT, axis=0, keepdims=True)

    def produce(k_blks, par):
        return tuple(produce_head(g, k_blks[g], par) for g in range(ATT_G))

    def pv_head(g, v_blk, alpha):
        acc_ref[g] = alpha * acc_ref[g] + _dot(v_blk, p_ref[g])

    def pv(v_blks, alphas):
        for g in range(ATT_G):
            pv_head(g, v_blks[g], alphas[g])

    def consume_head(g, j, sT, mblk, m):
        off = slopes[g] * (j * tk - qi * tq).astype(F32)
        m_new = jnp.maximum(m, mblk + off)
        p_ref[g] = jnp.exp2(sT - (m_new - off)).astype(BF16)
        return m_new, jnp.exp2(m - m_new)

    def consume(j, sTs, mblks, ms):
        outs = [consume_head(g, j, sTs[g], mblks[g], ms[g]) for g in range(ATT_G)]
        return tuple(o[0] for o in outs), tuple(o[1] for o in outs)

    def half_step(j, par, ms, alphas, mblks):
        k_blks = load_k(j + 1)
        v_blks = load_v(jnp.maximum(j - 1, 0))
        mblks_next = produce(k_blks, 1 - par)
        pv(v_blks, alphas)
        ms, alphas = consume(j, [s_ref[par, g] for g in range(ATT_G)], mblks, ms)
        return ms, alphas, mblks_next

    n_full = (qi * tq) // tk

    def pair(j, carry):
        carry = half_step(j, 0, *carry)
        return half_step(j + 1, 1, *carry)

    def quad(j, carry):
        return pair(j + 2, pair(j, carry))

    def body(t, carry):
        return quad(8 * t + 4, quad(8 * t, carry))

    def produce_block0():
        for g, mblk in enumerate(produce(load_k(0), 0)):
            mb_ref[g] = mblk

    @pl.when(qi == 0)
    def _():
        krow = lax.broadcasted_iota(jnp.int32, (tk, 128), 0)
        klane = lax.broadcasted_iota(jnp.int32, (tk, 128), 1)
        feat = jnp.where(klane < BIAS_TERMS, krow >> 4,
                         jnp.where(klane < 2 * BIAS_TERMS, krow & 15, 0))
        feat_ref[...] = feat.astype(F32).astype(BF16)
        brow = lax.broadcasted_iota(jnp.int32, (128, 2 * tq), 0)
        for g in range(ATT_G):
            bias_rows = jnp.zeros((128, 2 * tq), F32)
            for t in range(BIAS_TERMS):
                bias_rows = jnp.where(brow == t, 16.0 * parts[g][t], bias_rows)
                bias_rows = jnp.where(brow == BIAS_TERMS + t, parts[g][t], bias_rows)
            qs_ref[g, 128:256, :] = bias_rows.astype(BF16)
        load_queries(qT_ref)
        produce_block0()
        acc_ref[...] = jnp.zeros_like(acc_ref)
        p_ref[...] = jnp.zeros_like(p_ref)

    init = (tuple(jnp.full((1, 2 * tq), NEG_BIG, F32) for _ in range(ATT_G)),
            tuple(jnp.zeros((1, 2 * tq), F32) for _ in range(ATT_G)),
            tuple(mb_ref[g] for g in range(ATT_G)))
    carry = lax.fori_loop(0, n_full // 8, body, init)
    carry = lax.cond(n_full % 8 >= 4, lambda c: quad((n_full // 8) * 8, c), lambda c: c, carry)
    carry = lax.cond(n_full % 4 >= 2, lambda c: pair((n_full // 4) * 4, c), lambda c: c, carry)
    ms, alphas, _ = lax.cond(n_full % 2 == 1,
                             lambda c: half_step(n_full - 1, 0, *c), lambda c: c, carry)

    pv(load_v(jnp.maximum(n_full - 1, 0)), alphas)
    kpos = lax.broadcasted_iota(jnp.int32, (tk, 2 * tq), 0)
    qcol = lax.broadcasted_iota(jnp.int32, (tk, 2 * tq), 1)
    qpos = jnp.where(qcol >= tq, qcol - tq, qcol) + (qi * tq - n_full * tk)
    causal = kpos <= qpos
    par_last = n_full % 2
    load_queries(qTn_ref)
    k0_blks = load_k(0)
    v_blks = load_v(n_full)
    for g in range(ATT_G):
        sT = jnp.where(causal, s_ref[par_last, g], NEG_BIG)
        _, alpha = consume_head(g, n_full, sT, jnp.max(sT, axis=0, keepdims=True), ms[g])
        mb_ref[g] = produce_head(g, k0_blks[g], 0)
        pv_head(g, v_blks[g], alpha)

    lam = (jnp.exp(jnp.sum(lq1_ref[...] * lk1_ref[...], axis=-1, keepdims=True))
           - jnp.exp(jnp.sum(lq2_ref[...] * lk2_ref[...], axis=-1, keepdims=True))
           + LAMBDA_INIT)
    for g in range(ATT_G):
        acc = acc_ref[g]
        inv_l = 1.0 / acc[V_HEAD_DIM:V_HEAD_DIM + 1, :]
        num = acc[:V_HEAD_DIM, :] * inv_l
        oT = num[:, :tq] - lam * num[:, tq:]
        o = _rms(oT.T, g_ref[...]) * (1.0 - LAMBDA_INIT)
        o_ref[:, g * 128:(g + 1) * 128] = o.astype(BF16)


def _cast_view(w, n_chunks):
    cols = w.shape[-1]
    rows = w.size // cols
    while rows % (n_chunks * 16):
        assert cols % 256 == 0, w.shape
        cols //= 2
        rows *= 2
    return w.reshape(rows, cols)


def _attention(cparts, qT, k, vT, lq1, lk1, lq2, lk2, subln_g, cast_ws):
    S = k.shape[0]
    nq, nk = S // ATT_TQ, S // ATT_TK
    G = ATT_G
    n_steps = (N_HEADS // G) * nq
    vec = lambda h, i: (0, 0)
    once = pl.Buffered(1)
    views = [_cast_view(w, n_steps) for w in cast_ws]
    cast_specs = [pl.BlockSpec((v.shape[0] // n_steps, v.shape[1]), lambda h, i: (h * nq + i, 0))
                  for v in views]
    outs = pl.pallas_call(
        functools.partial(_attn_kernel, n_cast=len(views)),
        grid=(N_HEADS // G, nq),
        in_specs=[
            pl.BlockSpec(memory_space=pltpu.SMEM),
            pl.BlockSpec((G, None, 128, ATT_TQ), lambda h, i: (h, i, 0, 0)),
            pl.BlockSpec((G, None, 128, ATT_TQ), lambda h, i: (h, jnp.minimum(i + 1, nq - 1), 0, 0)),
            pl.BlockSpec((S, G * 128), lambda h, i: (0, h), pipeline_mode=once),
            pl.BlockSpec((G, nk, V_ROWS, ATT_TK), lambda h, i: (h, 0, 0, 0), pipeline_mode=once),
            pl.BlockSpec((1, QK_HEAD_DIM), vec),
            pl.BlockSpec((1, QK_HEAD_DIM), vec),
            pl.BlockSpec((1, QK_HEAD_DIM), vec),
            pl.BlockSpec((1, QK_HEAD_DIM), vec),
            pl.BlockSpec((1, V_HEAD_DIM), vec),
        ] + cast_specs,
        out_specs=[pl.BlockSpec((ATT_TQ, G * 128), lambda h, i: (i, h))] + cast_specs,
        out_shape=[jax.ShapeDtypeStruct((S, ATT_WIDTH), BF16)]
                  + [jax.ShapeDtypeStruct(v.shape, BF16) for v in views],
        scratch_shapes=[
            pltpu.VMEM((G, 256, 2 * ATT_TQ), BF16),
            pltpu.VMEM((G, V_ROWS, 2 * ATT_TQ), F32),
            pltpu.VMEM((G, ATT_TK, 2 * ATT_TQ), BF16),
            pltpu.VMEM((ATT_TK, 128), BF16),
            pltpu.VMEM((2, G, ATT_TK, 2 * ATT_TQ), F32),
            pltpu.VMEM((G, 1, 2 * ATT_TQ), F32),
        ],
        compiler_params=_params(("arbitrary", "arbitrary")),
        name="diff_attn",
    )(cparts, qT, qT, k, vT, lq1, lk1, lq2, lk2, subln_g, *views)
    return outs[0], [wb.reshape(w.shape) for wb, w in zip(outs[1:], cast_ws)]


def _mix_kernel(o_ref, u_ref, uprev_ref, gate_ref, x_ref, wg_ref, ps_ref, wa_ref, wp_ref, wo_ref,
                gm_ref, x1_ref, hm_ref, *, bm):
    i = pl.program_id(0)
    u = u_ref[...]
    halo = uprev_ref[...]
    ext = jnp.concatenate([jnp.where(i == 0, jnp.zeros_like(halo), halo), u], axis=0)
    t = i * bm + lax.broadcasted_iota(jnp.int32, (bm, 1), 0)
    pms = []
    a_chunks = []
    n_groups = len(POOL_WINDOWS)
    a_cols = D_MODEL // n_groups
    o = o_ref[...]
    for g, w in enumerate(POOL_WINDOWS):
        a_chunks.append(_dot(o, wa_ref[:, g * a_cols:(g + 1) * a_cols]))
        c0, c1 = g * POOL_GROUP_WIDTH, (g + 1) * POOL_GROUP_WIDTH
        ug = u[:, c0:c1]
        acc = ext[:, c0:c1]
        span = 1
        while span < w:
            acc = acc + pltpu.roll(acc, span, axis=0)
            span *= 2
        win = acc[POOL_HALO:, :]
        count = jnp.minimum(t + 1, w).astype(F32)
        pooled = win / count - ug
        pms.append(_dot(pooled.astype(BF16), wg_ref[g]))
    pm = (jnp.concatenate(pms, axis=-1) * ps_ref[...]).astype(BF16)
    p_branch = _dot(pm, wp_ref[...])
    a_branch = jnp.concatenate(a_chunks, axis=-1)
    gates = gate_ref[...]
    merged = gates[:, :D_MODEL].astype(F32) * a_branch + gates[:, D_MODEL:].astype(F32) * p_branch
    x1 = x_ref[...] + _dot(merged.astype(BF16), wo_ref[...])
    x1_ref[...] = x1
    hm_ref[...] = _rms(x1, gm_ref[...]).astype(BF16)


def _mix(o, u, gates, x, wg, pool_scale, wa, wp, wo, g_mlp, bm=256):
    S = x.shape[0]
    const2 = lambda i: (0, 0)
    once = pl.Buffered(1)
    halo_blocks = bm // POOL_HALO
    return pl.pallas_call(
        functools.partial(_mix_kernel, bm=bm),
        grid=(S // bm,),
        in_specs=[
            pl.BlockSpec((bm, ATT_WIDTH), lambda i: (i, 0)),
            pl.BlockSpec((bm, POOL_WIDTH), lambda i: (i, 0)),
            pl.BlockSpec((POOL_HALO, POOL_WIDTH), lambda i: (jnp.maximum(i * halo_blocks - 1, 0), 0)),
            pl.BlockSpec((bm, 2 * D_MODEL), lambda i: (i, 0)),
            pl.BlockSpec((bm, D_MODEL), lambda i: (i, 0)),
            pl.BlockSpec((len(POOL_WINDOWS), POOL_GROUP_WIDTH, POOL_GROUP_WIDTH), lambda i: (0, 0, 0),
                         pipeline_mode=once),
            pl.BlockSpec((1, POOL_WIDTH), const2),
            pl.BlockSpec((ATT_WIDTH, D_MODEL), const2, pipeline_mode=once),
            pl.BlockSpec((POOL_WIDTH, D_MODEL), const2, pipeline_mode=once),
            pl.BlockSpec((D_MODEL, D_MODEL), const2, pipeline_mode=once),
            pl.BlockSpec((1, D_MODEL), const2),
        ],
        out_specs=[
            pl.BlockSpec((bm, D_MODEL), lambda i: (i, 0)),
            pl.BlockSpec((bm, D_MODEL), lambda i: (i, 0)),
        ],
        out_shape=[
            jax.ShapeDtypeStruct((S, D_MODEL), F32),
            jax.ShapeDtypeStruct((S, D_MODEL), BF16),
        ],
        compiler_params=_params(("arbitrary",)),
        name="mix_out",
    )(o, u, u, gates, x, wg, pool_scale, wa, wp, wo, g_mlp)


def _mlp_step(hm_ref, wu_ref, wd_ref, d_ref, *, step_ref, n_f):
    step = step_ref[0]
    step_ref[0] = step + 1

    @pl.when(step % n_f == 0)
    def _():
        d_ref[...] = jnp.zeros_like(d_ref)

    a = jnp.maximum(_dot(hm_ref[...], wu_ref[...]), 0.0)
    d_ref[...] += _dot((a * a).astype(BF16), wd_ref[...])


def _mlp(hm, wu, wd, bm=1024, bf=1024):
    S = hm.shape[0]
    deep = pl.Buffered(3)

    def kernel_body(hm_hbm, wu_hbm, wd_hbm, d_hbm, step_ref):
        step_ref[0] = 0
        pltpu.emit_pipeline(
            functools.partial(_mlp_step, step_ref=step_ref, n_f=D_FF // bf),
            grid=(S // bm, D_FF // bf),
            in_specs=[
                pl.BlockSpec((bm, D_MODEL), lambda i, f: (i, 0)),
                pl.BlockSpec((D_MODEL, bf), lambda i, f: (0, f), pipeline_mode=deep),
                pl.BlockSpec((bf, D_MODEL), lambda i, f: (f, 0), pipeline_mode=deep),
            ],
            out_specs=[pl.BlockSpec((bm, D_MODEL), lambda i, f: (i, 0))],
        )(hm_hbm, wu_hbm, wd_hbm, d_hbm)

    any_spec = pl.BlockSpec(memory_space=pl.ANY)
    return pl.pallas_call(
        kernel_body,
        in_specs=[any_spec, any_spec, any_spec],
        out_specs=any_spec,
        out_shape=jax.ShapeDtypeStruct((S, D_MODEL), F32),
        scratch_shapes=[pltpu.SMEM((1,), jnp.int32)],
        compiler_params=pltpu.CompilerParams(vmem_limit_bytes=INPROJ_VMEM_LIMIT),
        name="mlp",
    )(hm, wu, wd)


PLE_CHUNK = 512


def _ple_kernel(x1_ref, d_ref, p_ref, gp_ref, wgate_ref, wple_ref, gf_ref, out_ref, *, bm):
    n_chunks = D_MODEL // PLE_CHUNK
    halves = [slice(0, bm // 2), slice(bm // 2, bm)]

    def prenorm(rows):
        x2 = x1_ref[rows, :] + d_ref[rows, :]
        return x2, _rms(x2, gp_ref[...]).astype(BF16), p_ref[rows, :].astype(BF16)

    def chunk(state, c):
        x2, hp, pb = state
        cols = slice(c * PLE_CHUNK, (c + 1) * PLE_CHUNK)
        gate = _sigmoid(_dot(hp, wgate_ref[:, cols]))
        return x2[:, cols] + _dot(pb, wple_ref[:, cols]) * gate

    def finish(rows, x3):
        out_ref[rows, :] = _rms(jnp.concatenate(x3, axis=-1), gf_ref[...])

    a = prenorm(halves[0])
    xa = [chunk(a, 0)]
    b = prenorm(halves[1])
    xa += [chunk(a, c) for c in range(1, n_chunks)]
    xb = [chunk(b, 0)]
    finish(halves[0], xa)
    xb += [chunk(b, c) for c in range(1, n_chunks)]
    finish(halves[1], xb)


def _ple(x1, d, p, g_ple, wgate, wple, g_final, bm=512):
    S = x1.shape[0]
    const2 = lambda i: (0, 0)
    once = pl.Buffered(1)
    return pl.pallas_call(
        functools.partial(_ple_kernel, bm=bm),
        grid=(S // bm,),
        in_specs=[
            pl.BlockSpec((bm, D_MODEL), lambda i: (i, 0)),
            pl.BlockSpec((bm, D_MODEL), lambda i: (i, 0)),
            pl.BlockSpec((None, None, bm, PLE_DIM), lambda i: (0, 0, i, 0)),
            pl.BlockSpec((1, D_MODEL), const2),
            pl.BlockSpec((D_MODEL, D_MODEL), const2, pipeline_mode=once),
            pl.BlockSpec((PLE_DIM, D_MODEL), const2, pipeline_mode=once),
            pl.BlockSpec((1, D_MODEL), const2),
        ],
        out_specs=pl.BlockSpec((bm, D_MODEL), lambda i: (i, 0)),
        out_shape=jax.ShapeDtypeStruct((S, D_MODEL), F32),
        compiler_params=_params(("arbitrary",)),
        name="ple_final",
    )(x1, d, p, g_ple, wgate, wple, g_final)


def kernel(x, p, norm_mix_g, w_in, lambda_q1, lambda_k1, lambda_q2, lambda_k2, subln_g, pool_grp_w, pool_scale, w_attn_br, w_pool_br, w_out, norm_mlp_g, w_mlp_up, w_mlp_down, norm_ple_g, w_ple, w_ple_gate, final_norm_g):
    B, S, D = x.shape
    assert (B, S, D) == (1, SEQ, D_MODEL) and norm_mix_g.shape[0] == 1
    cparts = jnp.asarray(_alibi_parts())

    x2d = x[0]
    h = _prenorm(x2d, norm_mix_g)
    qT, k, vT, u, gates = _inproj(h, w_in[0])
    o, (wg, wa, wp, wo, wu, wd, wgate, wple) = _attention(
        cparts, qT, k, vT, lambda_q1, lambda_k1, lambda_q2, lambda_k2, subln_g,
        [pool_grp_w[0], w_attn_br[0], w_pool_br[0], w_out[0], w_mlp_up[0], w_mlp_down[0],
         w_ple_gate[0], w_ple[0]])
    x1, hm = _mix(o, u, gates, x2d, wg, pool_scale, wa, wp, wo, norm_mlp_g)
    d = _mlp(hm, wu, wd)
    out = _ple(x1, d, p, norm_ple_g, wgate, wple, final_norm_g.reshape(1, D_MODEL))
    return out[None]
```
